```python
import math
import jax, jax.numpy as jnp
from jax import lax
import numpy as np

D_MODEL = 1024
BATCH = 4
SEQ = 4096
DEPTH = 1

CTX_LEN = 256
GRID_W = 64
D_ATTN = 512
N_HEADS_A = 8
HEAD_DIM = D_ATTN // N_HEADS_A
D_CONV = D_MODEL - D_ATTN
CONV_WIDTH = 31
NA_ROWS = 8
NA_COLS = 16
Q_BLOCK_COLS = 16
K_BLOCK_COLS = Q_BLOCK_COLS + NA_COLS
D_FF = 2816
FFN_CONV_WIDTH = 3
EPS = 1e-6
ATTN_SCALE = HEAD_DIM ** -0.5
SPLITS = [D_ATTN, 2 * D_ATTN, 3 * D_ATTN, 3 * D_ATTN + D_CONV]

kernel_name = 'hybrid_natten_conformer_dit_layer'


def rms_norm(x, g):
    xf = x.astype(jnp.float32)
    y = xf * lax.rsqrt(jnp.mean(xf * xf, axis=-1, keepdims=True) + EPS)
    return (y * g.astype(jnp.float32)).astype(x.dtype)


def layer_norm(x, g, b):
    xf = x.astype(jnp.float32)
    mu = jnp.mean(xf, axis=-1, keepdims=True)
    xc = xf - mu
    var = jnp.mean(xc * xc, axis=-1, keepdims=True)
    y = xc * lax.rsqrt(var + EPS) * g.astype(jnp.float32) + b.astype(jnp.float32)
    return y.astype(x.dtype)


def modulate(h, shift, scale):
    return h * (1 + scale) + shift


def depthwise_conv(x, w, b):
    y = lax.conv_general_dilated(x, w[:, None, :], window_strides=(1,), padding='SAME',
                                 dimension_numbers=('NWC', 'WIO', 'NWC'),
                                 feature_group_count=x.shape[-1])
    return y + b


def heads(t):
    return t.reshape(t.shape[0], t.shape[1], N_HEADS_A, HEAD_DIM)


def neighbourhood_tables(rows):
    wr = min(NA_ROWS, rows)
    n_blk = GRID_W // Q_BLOCK_COLS
    r = np.arange(rows)
    key_rows = np.clip(r - wr // 2, 0, rows - wr)[:, None] + np.arange(wr)[None, :]
    blk = np.arange(n_blk)
    key_cols = (np.clip(blk * Q_BLOCK_COLS - NA_COLS // 2, 0, GRID_W - K_BLOCK_COLS)[:, None]
                + np.arange(K_BLOCK_COLS)[None, :])
    q_cols = blk[:, None] * Q_BLOCK_COLS + np.arange(Q_BLOCK_COLS)[None, :]
    col_start = np.clip(q_cols - NA_COLS // 2, 0, GRID_W - NA_COLS)
    kc = key_cols[:, None, :]
    valid = (kc >= col_start[..., None]) & (kc < col_start[..., None] + NA_COLS)
    key_tok = key_rows[:, None, :, None] * GRID_W + key_cols[None, :, None, :]
    row_off = key_rows - r[:, None] + (NA_ROWS - 1)
    col_off = np.clip(kc - q_cols[..., None] + (NA_COLS - 1), 0, 2 * NA_COLS - 2)
    shape5 = (rows, n_blk, Q_BLOCK_COLS, wr, K_BLOCK_COLS)
    n_keys = wr * K_BLOCK_COLS
    shape4 = (rows, n_blk, Q_BLOCK_COLS, n_keys)
    valid = np.broadcast_to(valid[None, :, :, None, :], shape5).reshape(shape4)
    row_idx = np.broadcast_to(row_off[:, None, None, :, None], shape5).reshape(shape4)
    col_idx = np.broadcast_to(col_off[None, :, :, None, :], shape5).reshape(shape4)
    return (jnp.asarray(key_tok.reshape(rows, n_blk, n_keys), jnp.int32),
            jnp.asarray(row_idx, jnp.int32), jnp.asarray(col_idx, jnp.int32), jnp.asarray(valid))


def neighbourhood_bias(rpb, row_idx, col_idx, valid):
    b = rpb[:, row_idx, col_idx].astype(jnp.float32)
    b = jnp.where(valid[None], b, -jnp.inf)
    return jnp.transpose(b, (1, 2, 0, 3, 4))


def neighbourhood_attention(q, k, v, k_ctx, v_ctx, key_tok, bias):
    B, S = q.shape[0], q.shape[1]
    rows, n_blk, n_keys = key_tok.shape
    qb = q.reshape(B, rows, n_blk, Q_BLOCK_COLS, N_HEADS_A, HEAD_DIM)
    kg = k[:, key_tok]
    vg = v[:, key_tok]
    s_loc = jnp.einsum('brnqhd,brnkhd->brnhqk', qb, kg, preferred_element_type=jnp.float32) * ATTN_SCALE + bias
    s_ctx = jnp.einsum('brnqhd,bchd->brnhqc', qb, k_ctx, preferred_element_type=jnp.float32) * ATTN_SCALE
    p = jax.nn.softmax(jnp.concatenate([s_loc, s_ctx], axis=-1), axis=-1).astype(v.dtype)
    out = (jnp.einsum('brnhqk,brnkhd->brnqhd', p[..., :n_keys], vg)
           + jnp.einsum('brnhqc,bchd->brnqhd', p[..., n_keys:], v_ctx))
    return out.reshape(B, S, D_ATTN)


def context_attention(q, k, v):
    s = jnp.einsum('bqhd,bkhd->bhqk', q, k, preferred_element_type=jnp.float32) * ATTN_SCALE
    p = jax.nn.softmax(s, axis=-1).astype(v.dtype)
    out = jnp.einsum('bhqk,bkhd->bqhd', p, v)
    return out.reshape(q.shape[0], q.shape[1], D_ATTN)


def conformer_conv(a, g, conv_w, conv_b, ln_g, ln_b):
    u = a * jax.nn.sigmoid(g)
    u = depthwise_conv(u, conv_w, conv_b)
    u = layer_norm(u, ln_g, ln_b)
    return jax.nn.silu(u)


def conv_ffn(h, w_up, ffn_w, ffn_b, w_down):
    u = depthwise_conv(h @ w_up, ffn_w, ffn_b)
    gate, val = jnp.split(u, 2, axis=-1)
    return (jax.nn.silu(gate) * val) @ w_down


def setup_inputs(seed: int = 0) -> dict:
    key = jax.random.key(seed)
    ks = jax.random.split(key, 20)
    D = D_MODEL
    n_in = 3 * D_ATTN + 2 * D_CONV

    def nrm(k, shape, scale):
        return jax.random.normal(k, shape, jnp.float32) * scale

    return {
        'x': nrm(ks[0], (BATCH, SEQ, D), 1.0),
        'c': nrm(ks[1], (BATCH, D), 1.0),
        'ctx': nrm(ks[2], (BATCH, CTX_LEN, D), 1.0),
        'c_ctx': nrm(ks[3], (D,), 1.0),
        'w_mod': nrm(ks[4], (DEPTH, D, 6 * D), D ** -0.5),
        'b_mod': nrm(ks[5], (DEPTH, 6 * D), 0.02),
        'g_norm1': 1.0 + nrm(ks[6], (DEPTH, D), 0.05),
        'w_in': nrm(ks[7], (DEPTH, D, n_in), D ** -0.5),
        'rpb': nrm(ks[8], (DEPTH, N_HEADS_A, 2 * NA_ROWS - 1, 2 * NA_COLS - 1), 0.5),
        'conv_w': nrm(ks[9], (DEPTH, CONV_WIDTH, D_CONV), CONV_WIDTH ** -0.5),
        'conv_b': nrm(ks[10], (DEPTH, D_CONV), 0.02),
        'ln_g': 1.0 + nrm(ks[11], (DEPTH, D_CONV), 0.05),
        'ln_b': nrm(ks[12], (DEPTH, D_CONV), 0.02),
        'w_out': nrm(ks[13], (DEPTH, D, D), D ** -0.5),
        'g_norm2': 1.0 + nrm(ks[14], (DEPTH, D), 0.05),
        'w_up': nrm(ks[15], (DEPTH, D, 2 * D_FF), D ** -0.5),
        'ffn_conv_w': nrm(ks[16], (DEPTH, FFN_CONV_WIDTH, 2 * D_FF), FFN_CONV_WIDTH ** -0.5),
        'ffn_conv_b': nrm(ks[17], (DEPTH, 2 * D_FF), 0.02),
        'w_down': nrm(ks[18], (DEPTH, D_FF, D), D_FF ** -0.5),
        'g_final': 1.0 + nrm(ks[19], (D,), 0.05),
    }


def reference(x, c, ctx, c_ctx, w_mod, b_mod, g_norm1, w_in, rpb, conv_w, conv_b, ln_g, ln_b,
              w_out, g_norm2, w_up, ffn_conv_w, ffn_conv_b, w_down, g_final):
    S = x.shape[1]
    rows = S // GRID_W
    key_tok, row_idx, col_idx, valid = neighbourhood_tables(rows)
    c_act = jax.nn.silu(c)
    cctx_act = jax.nn.silu(c_ctx)
    for l in range(DEPTH):
        last = l == DEPTH - 1
        mod = (c_act @ w_mod[l] + b_mod[l])[:, None, :]
        sh1, sc1, gt1, sh2, sc2, gt2 = jnp.split(mod, 6, axis=-1)
        mod_c = cctx_act @ w_mod[l] + b_mod[l]
        csh1, csc1, cgt1, csh2, csc2, cgt2 = jnp.split(mod_c, 6, axis=-1)
        bias = neighbourhood_bias(rpb[l], row_idx, col_idx, valid)

        h = modulate(rms_norm(x, g_norm1[l]), sh1, sc1)
        hc = modulate(rms_norm(ctx, g_norm1[l]), csh1, csc1)
        q, k, v, a, g = jnp.split(h @ w_in[l], SPLITS, axis=-1)
        if last:
            k_c, v_c = jnp.split(hc @ w_in[l][:, D_ATTN:3 * D_ATTN], 2, axis=-1)
        else:
            q_c, k_c, v_c, a_c, g_c = jnp.split(hc @ w_in[l], SPLITS, axis=-1)
        k_c, v_c = heads(k_c), heads(v_c)
        y_na = neighbourhood_attention(heads(q), heads(k), heads(v), k_c, v_c, key_tok, bias)
        y_cv = conformer_conv(a, g, conv_w[l], conv_b[l], ln_g[l], ln_b[l])
        x = x + gt1 * (jnp.concatenate([y_na, y_cv], axis=-1) @ w_out[l])

        if not last:
            yc_na = context_attention(heads(q_c), k_c, v_c)
            yc_cv = conformer_conv(a_c, g_c, conv_w[l], conv_b[l], ln_g[l], ln_b[l])
            ctx = ctx + cgt1 * (jnp.concatenate([yc_na, yc_cv], axis=-1) @ w_out[l])
            hc2 = modulate(rms_norm(ctx, g_norm2[l]), csh2, csc2)
            ctx = ctx + cgt2 * conv_ffn(hc2, w_up[l], ffn_conv_w[l], ffn_conv_b[l], w_down[l])

        h2 = modulate(rms_norm(x, g_norm2[l]), sh2, sc2)
        x = x + gt2 * conv_ffn(h2, w_up[l], ffn_conv_w[l], ffn_conv_b[l], w_down[l])
    return rms_norm(x, g_final)
```

```python
import functools

import numpy as np
import jax
import jax.numpy as jnp
from jax import lax
from jax.experimental import pallas as pl
from jax.experimental.pallas import tpu as pltpu

F32 = jnp.float32
BF16 = jnp.bfloat16

GRID_W = 64
N_HEADS = 8
HEAD_DIM = 64
D_ATTN = N_HEADS * HEAD_DIM
CONV_WIDTH = 31
NA_ROWS = 8
NA_COLS = 16
FFN_CONV_WIDTH = 3
EPS = 1e-6
ATTN_SCALE = HEAD_DIM ** -0.5

HEADS_PER_GROUP = 4
GROUP_W = HEADS_PER_GROUP * HEAD_DIM
N_BIAS_CLASSES = NA_ROWS
CONV_HALO = 16
FFN_HALO = 8
CONV_ROW_BLOCK = 32
FF_CHUNK = 256

VMEM_LIMIT_BYTES = 56 * 1024 * 1024


def _sigmoid(x):
    return 1.0 / (1.0 + jnp.exp(-x))


def _norm_modulate(x, gain, shift, scale):
    ms = jnp.mean(x * x, axis=-1, keepdims=True)
    return (x * lax.rsqrt(ms + EPS)) * (gain * (1.0 + scale)) + shift


def _resident(shape):
    nd = len(shape)
    return pl.BlockSpec(shape, lambda *_: (0,) * nd, pipeline_mode=pl.Buffered(1))


def _mod_kernel(c_ref, w_ref, b_ref, o_ref):
    c = c_ref[...]
    a = (c * _sigmoid(c)).astype(BF16)
    o_ref[...] = jnp.dot(a, w_ref[...].astype(BF16), preferred_element_type=F32) + b_ref[...]


def _modulation(cc, w_mod, b_mod):
    rows, d = cc.shape
    n = w_mod.shape[1]
    tn = 1024
    return pl.pallas_call(
        _mod_kernel,
        grid=(n // tn,),
        in_specs=[
            pl.BlockSpec((rows, d), lambda j: (0, 0)),
            pl.BlockSpec((d, tn), lambda j: (0, j)),
            pl.BlockSpec((1, tn), lambda j: (0, j)),
        ],
        out_specs=pl.BlockSpec((rows, tn), lambda j: (0, j)),
        out_shape=jax.ShapeDtypeStruct((rows, n), F32),
        compiler_params=pltpu.CompilerParams(
            dimension_semantics=("arbitrary",), vmem_limit_bytes=VMEM_LIMIT_BYTES),
        name="modulation",
    )(cc, w_mod, b_mod)


def _inproj_kernel(x_ref, mod_ref, g_ref, w_ref, q_ref, k_ref, v_ref, u_ref):
    h = _norm_modulate(x_ref[0], g_ref[...], mod_ref[0, 0:1, :], mod_ref[0, 1:2, :]).astype(BF16)

    def proj(lo):
        return jnp.dot(h, w_ref[:, lo:lo + D_ATTN], preferred_element_type=F32)

    q_ref[0] = (proj(0) * ATTN_SCALE).astype(BF16)
    k_ref[0] = proj(D_ATTN).astype(BF16)
    v_ref[0] = proj(2 * D_ATTN).astype(BF16)
    a = proj(3 * D_ATTN)
    g = proj(4 * D_ATTN)
    u_ref[0] = (a * _sigmoid(g)).astype(BF16)


def _ctx_kv_kernel(x_ref, mod_ref, g_ref, w_ref, k_ref, v_ref):
    h = _norm_modulate(x_ref[0], g_ref[...], mod_ref[0, 0:1, :], mod_ref[0, 1:2, :]).astype(BF16)
    k_ref[0] = jnp.dot(h, w_ref[:, D_ATTN:2 * D_ATTN], preferred_element_type=F32).astype(BF16)
    v_ref[0] = jnp.dot(h, w_ref[:, 2 * D_ATTN:3 * D_ATTN], preferred_element_type=F32).astype(BF16)


def _input_projection(x, mod3, g1, w_in, tm):
    B, S, D = x.shape
    out = jax.ShapeDtypeStruct((B, S, D_ATTN), BF16)
    tile = pl.BlockSpec((1, tm, D_ATTN), lambda b, i: (b, i, 0))
    return pl.pallas_call(
        _inproj_kernel,
        grid=(B, S // tm),
        in_specs=[
            pl.BlockSpec((1, tm, D), lambda b, i: (b, i, 0)),
            pl.BlockSpec((1, 6, D), lambda b, i: (b, 0, 0)),
            _resident((1, D)),
            _resident(w_in.shape),
        ],
        out_specs=[tile, tile, tile, tile],
        out_shape=[out, out, out, out],
        compiler_params=pltpu.CompilerParams(
            dimension_semantics=("arbitrary", "arbitrary"), vmem_limit_bytes=VMEM_LIMIT_BYTES),
        name="input_projection",
    )(x, mod3, g1, w_in)


def _context_kv(ctx, mod3, g1, w_in, ctx_row):
    B, L, D = ctx.shape
    out = jax.ShapeDtypeStruct((B, L, D_ATTN), BF16)
    tile = pl.BlockSpec((1, L, D_ATTN), lambda b: (b, 0, 0))
    return pl.pallas_call(
        _ctx_kv_kernel,
        grid=(B,),
        in_specs=[
            pl.BlockSpec((1, L, D), lambda b: (b, 0, 0)),
            pl.BlockSpec((1, 6, D), lambda b: (ctx_row, 0, 0)),
            _resident((1, D)),
            _resident(w_in.shape),
        ],
        out_specs=[tile, tile],
        out_shape=[out, out],
        compiler_params=pltpu.CompilerParams(
            dimension_semantics=("arbitrary",), vmem_limit_bytes=VMEM_LIMIT_BYTES),
        name="context_kv",
    )(ctx, mod3, g1, w_in)


def _bias_classes(rpb):
    H, n_ro, n_co = rpb.shape
    half = n_co // 2
    period = 2 * GRID_W
    z = jnp.concatenate(
        [rpb[..., half:], jnp.zeros((H, n_ro, period - n_co), rpb.dtype), rpb[..., :half]], axis=-1)
    flat = jnp.tile(z, (1, 1, GRID_W))[..., :GRID_W * (period - 1)]
    toep = flat.reshape(H, n_ro, GRID_W, period - 1)[..., :GRID_W]
    q = np.arange(GRID_W)
    start = np.clip(q - NA_COLS // 2, 0, GRID_W - NA_COLS)
    kc = np.arange(GRID_W)
    valid = (kc[None, :] >= start[:, None]) & (kc[None, :] < start[:, None] + NA_COLS)
    toep = jnp.where(jnp.asarray(valid)[None, None], toep.astype(F32), -jnp.inf)
    classes = []
    for c in range(N_BIAS_CLASSES):
        t = toep[:, c:c + NA_ROWS]
        classes.append(jnp.transpose(t, (0, 2, 1, 3)).reshape(H, GRID_W, NA_ROWS * GRID_W))
    return jnp.stack(classes)


def _attn_kernel(q_ref, k_ref, v_ref, kc_ref, vc_ref, bias_ref, o_ref, *, rows_per_tile, n_rows):
    i = pl.program_id(1)
    n_local = NA_ROWS * GRID_W
    lane_head = lax.broadcasted_iota(jnp.int32, (GRID_W, GROUP_W), 1) // HEAD_DIM
    nt = (((1,), (1,)), ((), ()))
    for g in range(N_HEADS // HEADS_PER_GROUP):
        ls = slice(g * GROUP_W, (g + 1) * GROUP_W)
        kc = kc_ref[0, :, ls]
        vc = vc_ref[0, :, ls]

        def row_body(r, carry, ls=ls, kc=kc, vc=vc, g=g):
            r_abs = i * rows_per_tile + r
            kr0 = jnp.clip(r_abs - NA_ROWS // 2, 0, n_rows - NA_ROWS)
            cls = kr0 - r_abs + (NA_ROWS - 1)
            key0 = pl.multiple_of(kr0 * GRID_W, GRID_W)
            q0 = pl.multiple_of(r * GRID_W, GRID_W)
            qg = q_ref[0, pl.ds(q0, GRID_W), ls]
            kl = k_ref[0, pl.ds(key0, n_local), ls]
            vl = v_ref[0, pl.ds(key0, n_local), ls]
            qs = jnp.concatenate(
                [jnp.where(lane_head == hh, qg, 0) for hh in range(HEADS_PER_GROUP)], axis=0)
            s_loc = lax.dot_general(qs, kl, nt, preferred_element_type=F32)
            s_ctx = lax.dot_general(qs, kc, nt, preferred_element_type=F32)
            bias = bias_ref[cls, g * HEADS_PER_GROUP:(g + 1) * HEADS_PER_GROUP]
            s_loc = s_loc + bias.reshape(HEADS_PER_GROUP * GRID_W, n_local)
            m = jnp.maximum(jnp.max(s_loc, axis=-1, keepdims=True),
                            jnp.max(s_ctx, axis=-1, keepdims=True))
            p_loc = jnp.exp(s_loc - m)
            p_ctx = jnp.exp(s_ctx - m)
            denom = jnp.sum(p_loc, axis=-1, keepdims=True) + jnp.sum(p_ctx, axis=-1, keepdims=True)
            o = (jnp.dot(p_loc.astype(BF16), vl, preferred_element_type=F32)
                 + jnp.dot(p_ctx.astype(BF16), vc, preferred_element_type=F32))
            o = o * (1.0 / denom)
            y = o[0:GRID_W]
            for hh in range(1, HEADS_PER_GROUP):
                y = jnp.where(lane_head == hh, o[hh * GRID_W:(hh + 1) * GRID_W], y)
            o_ref[0, pl.ds(q0, GRID_W), ls] = y.astype(BF16)
            return carry

        lax.fori_loop(0, rows_per_tile, row_body, 0)


def _attention(q, k, v, kc, vc, bias, rows_per_tile):
    B, S, _ = q.shape
    L = kc.shape[1]
    n_rows = S // GRID_W
    tm = rows_per_tile * GRID_W
    return pl.pallas_call(
        functools.partial(_attn_kernel, rows_per_tile=rows_per_tile, n_rows=n_rows),
        grid=(B, n_rows // rows_per_tile),
        in_specs=[
            pl.BlockSpec((1, tm, D_ATTN), lambda b, i: (b, i, 0)),
            pl.BlockSpec((1, S, D_ATTN), lambda b, i: (b, 0, 0)),
            pl.BlockSpec((1, S, D_ATTN), lambda b, i: (b, 0, 0)),
            pl.BlockSpec((1, L, D_ATTN), lambda b, i: (b, 0, 0)),
            pl.BlockSpec((1, L, D_ATTN), lambda b, i: (b, 0, 0)),
            _resident(bias.shape),
        ],
        out_specs=pl.BlockSpec((1, tm, D_ATTN), lambda b, i: (b, i, 0)),
        out_shape=jax.ShapeDtypeStruct((B, S, D_ATTN), BF16),
        compiler_params=pltpu.CompilerParams(
            dimension_semantics=("arbitrary", "arbitrary"), vmem_limit_bytes=VMEM_LIMIT_BYTES),
        name="neighbourhood_attention",
    )(q, k, v, kc, vc, bias)


def _mix_out_kernel(yna_ref, u_ref, up_ref, un_ref, x_ref, mod_ref, cw_ref, cb_ref, lg_ref, lb_ref,
                    w_ref, o_ref, ext_ref, ush_ref, ycv_ref, *, tm):
    i = pl.program_id(1)
    last = pl.num_programs(1) - 1
    d_conv = u_ref.shape[-1]
    prev = up_ref[0].astype(F32)
    nxt = un_ref[0].astype(F32)
    ext_ref[0:CONV_HALO] = jnp.where(i > 0, prev, 0.0)
    ext_ref[CONV_HALO:CONV_HALO + tm] = u_ref[0].astype(F32)
    ext_ref[CONV_HALO + tm:] = jnp.where(i < last, nxt, 0.0)
    n_sh = tm + CONV_WIDTH - 7
    for s in range(8):
        ush_ref[s, 0:n_sh] = ext_ref[s:s + n_sh]

    cb = cb_ref[...]
    lg = lg_ref[...]
    lb = lb_ref[...]

    def block(rb, carry):
        r0 = pl.multiple_of(rb * CONV_ROW_BLOCK, CONV_ROW_BLOCK)
        acc = jnp.broadcast_to(cb, (CONV_ROW_BLOCK, d_conv))
        for j in range(CONV_WIDTH):
            off = j + CONV_HALO - CONV_WIDTH // 2
            s = off % 8
            acc = acc + cw_ref[j:j + 1, :] * ush_ref[s, pl.ds(r0 + (off - s), CONV_ROW_BLOCK), :]
        mu = jnp.mean(acc, axis=-1, keepdims=True)
        xc = acc - mu
        var = jnp.mean(xc * xc, axis=-1, keepdims=True)
        y = xc * lax.rsqrt(var + EPS) * lg + lb
        ycv_ref[pl.ds(r0, CONV_ROW_BLOCK), :] = (y * _sigmoid(y)).astype(BF16)
        return carry

    lax.fori_loop(0, tm // CONV_ROW_BLOCK, block, 0)

    d_attn = yna_ref.shape[-1]
    proj = (jnp.dot(yna_ref[0], w_ref[0:d_attn, :], preferred_element_type=F32)
            + jnp.dot(ycv_ref[...], w_ref[d_attn:, :], preferred_element_type=F32))
    o_ref[0] = x_ref[0] + mod_ref[0, 2:3, :] * proj


def _mix_out(y_na, u, x, mod3, conv_w, conv_b, ln_g, ln_b, w_out, tm):
    B, S, D = x.shape
    d_conv = u.shape[-1]
    hb = tm // CONV_HALO
    n_hb = S // CONV_HALO
    return pl.pallas_call(
        functools.partial(_mix_out_kernel, tm=tm),
        grid=(B, S // tm),
        in_specs=[
            pl.BlockSpec((1, tm, D_ATTN), lambda b, i: (b, i, 0)),
            pl.BlockSpec((1, tm, d_conv), lambda b, i: (b, i, 0)),
            pl.BlockSpec((1, CONV_HALO, d_conv), lambda b, i: (b, jnp.maximum(i * hb - 1, 0), 0)),
            pl.BlockSpec((1, CONV_HALO, d_conv), lambda b, i: (b, jnp.minimum((i + 1) * hb, n_hb - 1), 0)),
            pl.BlockSpec((1, tm, D), lambda b, i: (b, i, 0)),
            pl.BlockSpec((1, 6, D), lambda b, i: (b, 0, 0)),
            _resident(conv_w.shape),
            _resident((1, d_conv)),
            _resident((1, d_conv)),
            _resident((1, d_conv)),
            _resident(w_out.shape),
        ],
        out_specs=pl.BlockSpec((1, tm, D), lambda b, i: (b, i, 0)),
        out_shape=jax.ShapeDtypeStruct((B, S, D), F32),
        scratch_shapes=[
            pltpu.VMEM((tm + 2 * CONV_HALO, d_conv), F32),
            pltpu.VMEM((8, tm + 2 * CONV_HALO, d_conv), F32),
            pltpu.VMEM((tm, d_conv), BF16),
        ],
        compiler_params=pltpu.CompilerParams(
            dimension_semantics=("arbitrary", "arbitrary"), vmem_limit_bytes=VMEM_LIMIT_BYTES),
        name="conv_mixer_out_projection",
    )(y_na, u, u, u, x, mod3, conv_w, conv_b, ln_g, ln_b, w_out)


def _ffn_kernel(x_ref, xp_ref, xn_ref, mod_ref, g2_ref, wu_ref, fw_ref, fb_ref, wd_ref, gf_ref,
                o_ref, h_ref, act_ref, *, tm, d_ff):
    i = pl.program_id(1)
    last = pl.num_programs(1) - 1
    n_ext = tm + 2 * FFN_HALO
    gain = g2_ref[...]
    shift = mod_ref[0, 3:4, :]
    scale = mod_ref[0, 4:5, :]
    hp = _norm_modulate(xp_ref[0], gain, shift, scale)
    hn = _norm_modulate(xn_ref[0], gain, shift, scale)
    h_ref[0:FFN_HALO] = jnp.where(i > 0, hp, 0.0).astype(BF16)
    h_ref[FFN_HALO:FFN_HALO + tm] = _norm_modulate(x_ref[0], gain, shift, scale).astype(BF16)
    h_ref[FFN_HALO + tm:] = jnp.where(i < last, hn, 0.0).astype(BF16)

    def conv(lo):
        up = jnp.dot(h_ref[...], wu_ref[:, lo:lo + FF_CHUNK], preferred_element_type=F32)
        before = pltpu.roll(up, 1, axis=0)
        after = pltpu.roll(up, n_ext - 1, axis=0)
        mid = slice(FFN_HALO, FFN_HALO + tm)
        w = fw_ref[:, lo:lo + FF_CHUNK]
        return (w[0:1] * before[mid] + w[1:2] * up[mid] + w[2:3] * after[mid]
                + fb_ref[:, lo:lo + FF_CHUNK])

    for f in range(d_ff // FF_CHUNK):
        gate = conv(f * FF_CHUNK)
        val = conv(d_ff + f * FF_CHUNK)
        act_ref[:, f * FF_CHUNK:(f + 1) * FF_CHUNK] = (gate * _sigmoid(gate) * val).astype(BF16)

    y = jnp.dot(act_ref[...], wd_ref[...], preferred_element_type=F32)
    x2 = x_ref[0] + mod_ref[0, 5:6, :] * y
    ms = jnp.mean(x2 * x2, axis=-1, keepdims=True)
    o_ref[0] = x2 * lax.rsqrt(ms + EPS) * gf_ref[...]


def _conv_ffn(x1, mod3, g2, w_up, ffn_w, ffn_b, w_down, g_final, tm):
    B, S, D = x1.shape
    d_ff = w_down.shape[0]
    hb = tm // FFN_HALO
    n_hb = S // FFN_HALO
    return pl.pallas_call(
        functools.partial(_ffn_kernel, tm=tm, d_ff=d_ff),
        grid=(B, S // tm),
        in_specs=[
            pl.BlockSpec((1, tm, D), lambda b, i: (b, i, 0)),
            pl.BlockSpec((1, FFN_HALO, D), lambda b, i: (b, jnp.maximum(i * hb - 1, 0), 0)),
            pl.BlockSpec((1, FFN_HALO, D), lambda b, i: (b, jnp.minimum((i + 1) * hb, n_hb - 1), 0)),
            pl.BlockSpec((1, 6, D), lambda b, i: (b, 0, 0)),
            _resident((1, D)),
            _resident(w_up.shape),
            _resident(ffn_w.shape),
            _resident((1, 2 * d_ff)),
            _resident(w_down.shape),
            _resident((1, D)),
        ],
        out_specs=pl.BlockSpec((1, tm, D), lambda b, i: (b, i, 0)),
        out_shape=jax.ShapeDtypeStruct((B, S, D), F32),
        scratch_shapes=[
            pltpu.VMEM((tm + 2 * FFN_HALO, D), BF16),
            pltpu.VMEM((tm, d_ff), BF16),
        ],
        compiler_params=pltpu.CompilerParams(
            dimension_semantics=("arbitrary", "arbitrary"), vmem_limit_bytes=VMEM_LIMIT_BYTES),
        name="conv_ffn_final_norm",
    )(x1, x1, x1, mod3, g2, w_up, ffn_w, ffn_b, w_down, g_final)


def kernel(x, c, ctx, c_ctx, w_mod, b_mod, g_norm1, w_in, rpb, conv_w, conv_b, ln_g, ln_b, w_out,
           g_norm2, w_up, ffn_conv_w, ffn_conv_b, w_down, g_final):
    B, S, D = x.shape
    depth = w_mod.shape[0]
    assert depth == 1, "the context-stream update of deeper stacks is not implemented"
    assert S % GRID_W == 0 and D_ATTN + conv_w.shape[-1] == D
    l = 0
    tm = 512
    row = lambda a: a.reshape(1, -1)

    mod_rows = 8
    cc = jnp.concatenate([c, c_ctx[None], jnp.zeros((mod_rows - B - 1, D), c.dtype)], axis=0)
    mod3 = _modulation(cc, w_mod[l], row(b_mod[l])).reshape(mod_rows, 6, D)

    w_in_b = w_in[l].astype(BF16)
    g1 = row(g_norm1[l])
    q, k, v, u = _input_projection(x, mod3, g1, w_in_b, tm)
    k_c, v_c = _context_kv(ctx, mod3, g1, w_in_b, ctx_row=B)

    y_na = _attention(q, k, v, k_c, v_c, _bias_classes(rpb[l]), rows_per_tile=8)

    x1 = _mix_out(y_na, u, x, mod3, conv_w[l], row(conv_b[l]), row(ln_g[l]), row(ln_b[l]),
                  w_out[l].astype(BF16), tm)

    return _conv_ffn(x1, mod3, row(g_norm2[l]), w_up[l].astype(BF16), ffn_conv_w[l],
                     row(ffn_conv_b[l]), w_down[l].astype(BF16), row(g_final), tm)
```

```python
import functools

import numpy as np
import jax
import jax.numpy as jnp
from jax import lax
from jax.experimental import pallas as pl
from jax.experimental.pallas import tpu as pltpu

F32 = jnp.float32
BF16 = jnp.bfloat16

GRID_W = 64
N_HEADS = 8
HEAD_DIM = 64
D_ATTN = N_HEADS * HEAD_DIM
CONV_WIDTH = 31
NA_ROWS = 8
NA_COLS = 16
FFN_CONV_WIDTH = 3
EPS = 1e-6
ATTN_SCALE = HEAD_DIM ** -0.5

HEADS_PER_GROUP = 4
GROUP_W = HEADS_PER_GROUP * HEAD_DIM
N_BIAS_CLASSES = NA_ROWS
CONV_HALO = 16
FFN_HALO = 8
CONV_ROW_BLOCK = 32
MIX_ROW_BLOCK = 128
FF_CHUNK = 256
PROJ_TILE = 1024
MIX_TILE = 512
ATTN_ROWS_PER_TILE = 8
ATTN_ROW_UNROLL = 4

VMEM_LIMIT_BYTES = 56 * 1024 * 1024


def _sigmoid(x):
    return 1.0 / (1.0 + jnp.exp(-x))


def _norm_modulate(x, gain, shift, scale):
    ms = jnp.mean(x * x, axis=-1, keepdims=True)
    return (x * lax.rsqrt(ms + EPS)) * (gain * (1.0 + scale)) + shift


def _resident(shape):
    nd = len(shape)
    return pl.BlockSpec(shape, lambda *_: (0,) * nd, pipeline_mode=pl.Buffered(1))


def _mod_kernel(c_ref, w_ref, b_ref, o_ref):
    c = c_ref[...]
    a = (c * _sigmoid(c)).astype(BF16)
    o_ref[...] = jnp.dot(a, w_ref[...].astype(BF16), preferred_element_type=F32) + b_ref[...]


def _modulation(cc, w_mod, b_mod):
    rows, d = cc.shape
    n = w_mod.shape[1]
    tn = 1024
    return pl.pallas_call(
        _mod_kernel,
        grid=(n // tn,),
        in_specs=[
            pl.BlockSpec((rows, d), lambda j: (0, 0)),
            pl.BlockSpec((d, tn), lambda j: (0, j)),
            pl.BlockSpec((1, tn), lambda j: (0, j)),
        ],
        out_specs=pl.BlockSpec((rows, tn), lambda j: (0, j)),
        out_shape=jax.ShapeDtypeStruct((rows, n), F32),
        compiler_params=pltpu.CompilerParams(
            dimension_semantics=("arbitrary",), vmem_limit_bytes=VMEM_LIMIT_BYTES),
        name="modulation",
    )(cc, w_mod, b_mod)


def _inproj_kernel(x_ref, mod_ref, g_ref, w_ref, q_ref, k_ref, v_ref, u_ref):
    h = _norm_modulate(x_ref[0], g_ref[...], mod_ref[0, 0:1, :], mod_ref[0, 1:2, :]).astype(BF16)

    def proj(lo):
        return jnp.dot(h, w_ref[:, lo:lo + D_ATTN], preferred_element_type=F32)

    q_ref[0] = (proj(0) * ATTN_SCALE).astype(BF16)
    k_ref[0] = proj(D_ATTN).astype(BF16)
    v_ref[0] = proj(2 * D_ATTN).astype(BF16)
    a = proj(3 * D_ATTN)
    g = proj(4 * D_ATTN)
    u_ref[0] = (a * _sigmoid(g)).astype(BF16)


def _ctx_kv_kernel(x_ref, mod_ref, g_ref, w_ref, k_ref, v_ref):
    h = _norm_modulate(x_ref[0], g_ref[...], mod_ref[0, 0:1, :], mod_ref[0, 1:2, :]).astype(BF16)
    k_ref[0] = jnp.dot(h, w_ref[:, D_ATTN:2 * D_ATTN], preferred_element_type=F32).astype(BF16)
    v_ref[0] = jnp.dot(h, w_ref[:, 2 * D_ATTN:3 * D_ATTN], preferred_element_type=F32).astype(BF16)


def _input_projection(x, mod3, g1, w_in, tm):
    B, S, D = x.shape
    out = jax.ShapeDtypeStruct((B, S, D_ATTN), BF16)
    tile = pl.BlockSpec((1, tm, D_ATTN), lambda b, i: (b, i, 0))
    return pl.pallas_call(
        _inproj_kernel,
        grid=(B, S // tm),
        in_specs=[
            pl.BlockSpec((1, tm, D), lambda b, i: (b, i, 0)),
            pl.BlockSpec((1, 6, D), lambda b, i: (b, 0, 0)),
            _resident((1, D)),
            _resident(w_in.shape),
        ],
        out_specs=[tile, tile, tile, tile],
        out_shape=[out, out, out, out],
        compiler_params=pltpu.CompilerParams(
            dimension_semantics=("arbitrary", "arbitrary"), vmem_limit_bytes=VMEM_LIMIT_BYTES),
        name="input_projection",
    )(x, mod3, g1, w_in)


def _context_kv(ctx, mod3, g1, w_in, ctx_row):
    B, L, D = ctx.shape
    out = jax.ShapeDtypeStruct((B, L, D_ATTN), BF16)
    tile = pl.BlockSpec((1, L, D_ATTN), lambda b: (b, 0, 0))
    return pl.pallas_call(
        _ctx_kv_kernel,
        grid=(B,),
        in_specs=[
            pl.BlockSpec((1, L, D), lambda b: (b, 0, 0)),
            pl.BlockSpec((1, 6, D), lambda b: (ctx_row, 0, 0)),
            _resident((1, D)),
            _resident(w_in.shape),
        ],
        out_specs=[tile, tile],
        out_shape=[out, out],
        compiler_params=pltpu.CompilerParams(
            dimension_semantics=("arbitrary",), vmem_limit_bytes=VMEM_LIMIT_BYTES),
        name="context_kv",
    )(ctx, mod3, g1, w_in)


def _bias_classes(rpb):
    H, n_ro, n_co = rpb.shape
    half = n_co // 2
    period = 2 * GRID_W
    z = jnp.concatenate(
        [rpb[..., half:], jnp.zeros((H, n_ro, period - n_co), rpb.dtype), rpb[..., :half]], axis=-1)
    flat = jnp.tile(z, (1, 1, GRID_W))[..., :GRID_W * (period - 1)]
    toep = flat.reshape(H, n_ro, GRID_W, period - 1)[..., :GRID_W]
    q = np.arange(GRID_W)
    start = np.clip(q - NA_COLS // 2, 0, GRID_W - NA_COLS)
    kc = np.arange(GRID_W)
    valid = (kc[None, :] >= start[:, None]) & (kc[None, :] < start[:, None] + NA_COLS)
    toep = jnp.where(jnp.asarray(valid)[None, None], toep.astype(F32), -jnp.inf)
    ro_period = n_ro + 1
    padded = jnp.concatenate([toep, jnp.zeros((H, ro_period - n_ro, GRID_W, GRID_W), F32)], axis=1)
    reps = -(-N_BIAS_CLASSES * (ro_period + 1) // ro_period)
    hank = jnp.tile(padded, (1, reps, 1, 1))[:, :N_BIAS_CLASSES * (ro_period + 1)]
    hank = hank.reshape(H, N_BIAS_CLASSES, ro_period + 1, GRID_W, GRID_W)[:, :, :NA_ROWS]
    return jnp.transpose(hank, (1, 0, 3, 2, 4)).reshape(N_BIAS_CLASSES, H, GRID_W, NA_ROWS * GRID_W)


def _attn_kernel(q_ref, k_ref, v_ref, kc_ref, vc_ref, bias_ref, o_ref, *, rows_per_tile, n_rows):
    i = pl.program_id(1)
    n_local = NA_ROWS * GRID_W
    lane_head = lax.broadcasted_iota(jnp.int32, (GRID_W, GROUP_W), 1) // HEAD_DIM
    nt = (((1,), (1,)), ((), ()))
    def row_body(r, carry):
        r_abs = i * rows_per_tile + r
        kr0 = jnp.clip(r_abs - NA_ROWS // 2, 0, n_rows - NA_ROWS)
        cls = kr0 - r_abs + (NA_ROWS - 1)
        key0 = pl.multiple_of(kr0 * GRID_W, GRID_W)
        q0 = pl.multiple_of(r * GRID_W, GRID_W)
        for g in range(N_HEADS // HEADS_PER_GROUP):
            ls = slice(g * GROUP_W, (g + 1) * GROUP_W)
            kc = kc_ref[0, :, ls]
            vc = vc_ref[0, :, ls]
            qg = q_ref[0, pl.ds(q0, GRID_W), ls]
            kl = k_ref[0, pl.ds(key0, n_local), ls]
            vl = v_ref[0, pl.ds(key0, n_local), ls]
            qs = jnp.concatenate(
                [jnp.where(lane_head == hh, qg, 0) for hh in range(HEADS_PER_GROUP)], axis=0)
            s_loc = lax.dot_general(qs, kl, nt, preferred_element_type=F32)
            s_ctx = lax.dot_general(qs, kc, nt, preferred_element_type=F32)
            bias = bias_ref[cls, g * HEADS_PER_GROUP:(g + 1) * HEADS_PER_GROUP]
            s_loc = s_loc + bias.reshape(HEADS_PER_GROUP * GRID_W, n_local)
            m = jnp.maximum(jnp.max(s_loc, axis=-1, keepdims=True),
                            jnp.max(s_ctx, axis=-1, keepdims=True))
            p_loc = jnp.exp(s_loc - m)
            p_ctx = jnp.exp(s_ctx - m)
            denom = jnp.sum(p_loc, axis=-1, keepdims=True) + jnp.sum(p_ctx, axis=-1, keepdims=True)
            o = (jnp.dot(p_loc.astype(BF16), vl, preferred_element_type=F32)
                 + jnp.dot(p_ctx.astype(BF16), vc, preferred_element_type=F32))
            o = o * (1.0 / denom)
            y = o[0:GRID_W]
            for hh in range(1, HEADS_PER_GROUP):
                y = jnp.where(lane_head == hh, o[hh * GRID_W:(hh + 1) * GRID_W], y)
            o_ref[0, pl.ds(q0, GRID_W), ls] = y.astype(BF16)
        return carry

    lax.fori_loop(0, rows_per_tile, row_body, 0, unroll=ATTN_ROW_UNROLL)


def _attention(q, k, v, kc, vc, bias, rows_per_tile):
    B, S, _ = q.shape
    L = kc.shape[1]
    n_rows = S // GRID_W
    tm = rows_per_tile * GRID_W
    return pl.pallas_call(
        functools.partial(_attn_kernel, rows_per_tile=rows_per_tile, n_rows=n_rows),
        grid=(B, n_rows // rows_per_tile),
        in_specs=[
            pl.BlockSpec((1, tm, D_ATTN), lambda b, i: (b, i, 0)),
            pl.BlockSpec((1, S, D_ATTN), lambda b, i: (b, 0, 0)),
            pl.BlockSpec((1, S, D_ATTN), lambda b, i: (b, 0, 0)),
            pl.BlockSpec((1, L, D_ATTN), lambda b, i: (b, 0, 0)),
            pl.BlockSpec((1, L, D_ATTN), lambda b, i: (b, 0, 0)),
            _resident(bias.shape),
        ],
        out_specs=pl.BlockSpec((1, tm, D_ATTN), lambda b, i: (b, i, 0)),
        out_shape=jax.ShapeDtypeStruct((B, S, D_ATTN), BF16),
        compiler_params=pltpu.CompilerParams(
            dimension_semantics=("arbitrary", "arbitrary"), vmem_limit_bytes=VMEM_LIMIT_BYTES),
        name="neighbourhood_attention",
    )(q, k, v, kc, vc, bias)


def _mix_out_kernel(yna_ref, u_ref, up_ref, un_ref, x_ref, mod_ref, cw_ref, cb_ref, lg_ref, lb_ref,
                    w_ref, o_ref, ext_ref, ush_ref, *, tm):
    i = pl.program_id(1)
    last = pl.num_programs(1) - 1
    d_conv = u_ref.shape[-1]
    prev = up_ref[0].astype(F32)
    nxt = un_ref[0].astype(F32)
    ext_ref[0:CONV_HALO] = jnp.where(i > 0, prev, 0.0)
    ext_ref[CONV_HALO:CONV_HALO + tm] = u_ref[0].astype(F32)
    ext_ref[CONV_HALO + tm:] = jnp.where(i < last, nxt, 0.0)
    n_sh = tm + CONV_WIDTH - 7
    for s in range(8):
        ush_ref[s, 0:n_sh] = ext_ref[s:s + n_sh]

    cb = cb_ref[...]
    lg = lg_ref[...]
    lb = lb_ref[...]

    d_attn = yna_ref.shape[-1]
    gate = mod_ref[0, 2:3, :]

    def conv_rows(r0):
        acc = jnp.broadcast_to(cb, (CONV_ROW_BLOCK, d_conv))
        for j in range(CONV_WIDTH):
            off = j + CONV_HALO - CONV_WIDTH // 2
            s = off % 8
            acc = acc + cw_ref[j:j + 1, :] * ush_ref[s, pl.ds(r0 + (off - s), CONV_ROW_BLOCK), :]
        mu = jnp.mean(acc, axis=-1, keepdims=True)
        xc = acc - mu
        var = jnp.mean(xc * xc, axis=-1, keepdims=True)
        y = xc * lax.rsqrt(var + EPS) * lg + lb
        return (y * _sigmoid(y)).astype(BF16)

    def block(rb, carry):
        r0 = pl.multiple_of(rb * MIX_ROW_BLOCK, MIX_ROW_BLOCK)
        rows = pl.ds(r0, MIX_ROW_BLOCK)
        ycv = jnp.concatenate(
            [conv_rows(r0 + sb * CONV_ROW_BLOCK) for sb in range(MIX_ROW_BLOCK // CONV_ROW_BLOCK)], axis=0)
        proj = (jnp.dot(yna_ref[0, rows, :], w_ref[0:d_attn, :], preferred_element_type=F32)
                + jnp.dot(ycv, w_ref[d_attn:, :], preferred_element_type=F32))
        o_ref[0, rows, :] = x_ref[0, rows, :] + gate * proj
        return carry

    lax.fori_loop(0, tm // MIX_ROW_BLOCK, block, 0)


def _mix_out(y_na, u, x, mod3, conv_w, conv_b, ln_g, ln_b, w_out, tm):
    B, S, D = x.shape
    d_conv = u.shape[-1]
    hb = tm // CONV_HALO
    n_hb = S // CONV_HALO
    return pl.pallas_call(
        functools.partial(_mix_out_kernel, tm=tm),
        grid=(B, S // tm),
        in_specs=[
            pl.BlockSpec((1, tm, D_ATTN), lambda b, i: (b, i, 0)),
            pl.BlockSpec((1, tm, d_conv), lambda b, i: (b, i, 0)),
            pl.BlockSpec((1, CONV_HALO, d_conv), lambda b, i: (b, jnp.maximum(i * hb - 1, 0), 0)),
            pl.BlockSpec((1, CONV_HALO, d_conv), lambda b, i: (b, jnp.minimum((i + 1) * hb, n_hb - 1), 0)),
            pl.BlockSpec((1, tm, D), lambda b, i: (b, i, 0)),
            pl.BlockSpec((1, 6, D), lambda b, i: (b, 0, 0)),
            _resident(conv_w.shape),
            _resident((1, d_conv)),
            _resident((1, d_conv)),
            _resident((1, d_conv)),
            _resident(w_out.shape),
        ],
        out_specs=pl.BlockSpec((1, tm, D), lambda b, i: (b, i, 0)),
        out_shape=jax.ShapeDtypeStruct((B, S, D), F32),
        scratch_shapes=[
            pltpu.VMEM((tm + 2 * CONV_HALO, d_conv), F32),
            pltpu.VMEM((8, tm + 2 * CONV_HALO, d_conv), F32),
        ],
        compiler_params=pltpu.CompilerParams(
            dimension_semantics=("arbitrary", "arbitrary"), vmem_limit_bytes=VMEM_LIMIT_BYTES),
        name="conv_mixer_out_projection",
    )(y_na, u, u, u, x, mod3, conv_w, conv_b, ln_g, ln_b, w_out)


def _ffn_kernel(x_ref, xp_ref, xn_ref, mod_ref, g2_ref, wu_ref, fw_ref, fb_ref, wd_ref, gf_ref,
                o_ref, h_ref, act_ref, *, tm, d_ff):
    i = pl.program_id(1)
    last = pl.num_programs(1) - 1
    n_ext = tm + 2 * FFN_HALO
    gain = g2_ref[...]
    shift = mod_ref[0, 3:4, :]
    scale = mod_ref[0, 4:5, :]
    hp = _norm_modulate(xp_ref[0], gain, shift, scale)
    hn = _norm_modulate(xn_ref[0], gain, shift, scale)
    h_ref[0:FFN_HALO] = jnp.where(i > 0, hp, 0.0).astype(BF16)
    h_ref[FFN_HALO:FFN_HALO + tm] = _norm_modulate(x_ref[0], gain, shift, scale).astype(BF16)
    h_ref[FFN_HALO + tm:] = jnp.where(i < last, hn, 0.0).astype(BF16)

    def conv(lo):
        up = jnp.dot(h_ref[...], wu_ref[:, lo:lo + FF_CHUNK], preferred_element_type=F32)
        before = pltpu.roll(up, 1, axis=0)
        after = pltpu.roll(up, n_ext - 1, axis=0)
        mid = slice(FFN_HALO, FFN_HALO + tm)
        w = fw_ref[:, lo:lo + FF_CHUNK]
        return (w[0:1] * before[mid] + w[1:2] * up[mid] + w[2:3] * after[mid]
                + fb_ref[:, lo:lo + FF_CHUNK])

    for f in range(d_ff // FF_CHUNK):
        gate = conv(f * FF_CHUNK)
        val = conv(d_ff + f * FF_CHUNK)
        act_ref[:, f * FF_CHUNK:(f + 1) * FF_CHUNK] = (gate * _sigmoid(gate) * val).astype(BF16)

    y = jnp.dot(act_ref[...], wd_ref[...], preferred_element_type=F32)
    x2 = x_ref[0] + mod_ref[0, 5:6, :] * y
    ms = jnp.mean(x2 * x2, axis=-1, keepdims=True)
    o_ref[0] = x2 * lax.rsqrt(ms + EPS) * gf_ref[...]


def _conv_ffn(x1, mod3, g2, w_up, ffn_w, ffn_b, w_down, g_final, tm):
    B, S, D = x1.shape
    d_ff = w_down.shape[0]
    hb = tm // FFN_HALO
    n_hb = S // FFN_HALO
    return pl.pallas_call(
        functools.partial(_ffn_kernel, tm=tm, d_ff=d_ff),
        grid=(B, S // tm),
        in_specs=[
            pl.BlockSpec((1, tm, D), lambda b, i: (b, i, 0)),
            pl.BlockSpec((1, FFN_HALO, D), lambda b, i: (b, jnp.maximum(i * hb - 1, 0), 0)),
            pl.BlockSpec((1, FFN_HALO, D), lambda b, i: (b, jnp.minimum((i + 1) * hb, n_hb - 1), 0)),
            pl.BlockSpec((1, 6, D), lambda b, i: (b, 0, 0)),
            _resident((1, D)),
            _resident(w_up.shape),
            _resident(ffn_w.shape),
            _resident((1, 2 * d_ff)),
            _resident(w_down.shape),
            _resident((1, D)),
        ],
        out_specs=pl.BlockSpec((1, tm, D), lambda b, i: (b, i, 0)),
        out_shape=jax.ShapeDtypeStruct((B, S, D), F32),
        scratch_shapes=[
            pltpu.VMEM((tm + 2 * FFN_HALO, D), BF16),
            pltpu.VMEM((tm, d_ff), BF16),
        ],
        compiler_params=pltpu.CompilerParams(
            dimension_semantics=("arbitrary", "arbitrary"), vmem_limit_bytes=VMEM_LIMIT_BYTES),
        name="conv_ffn_final_norm",
    )(x1, x1, x1, mod3, g2, w_up, ffn_w, ffn_b, w_down, g_final)


def kernel(x, c, ctx, c_ctx, w_mod, b_mod, g_norm1, w_in, rpb, conv_w, conv_b, ln_g, ln_b, w_out,
           g_norm2, w_up, ffn_conv_w, ffn_conv_b, w_down, g_final):
    B, S, D = x.shape
    depth = w_mod.shape[0]
    assert depth == 1, "the context-stream update of deeper stacks is not implemented"
    assert S % GRID_W == 0 and D_ATTN + conv_w.shape[-1] == D
    l = 0
    row = lambda a: a.reshape(1, -1)

    mod_rows = 8
    cc = jnp.concatenate([c, c_ctx[None], jnp.zeros((mod_rows - B - 1, D), c.dtype)], axis=0)
    mod3 = _modulation(cc, w_mod[l], row(b_mod[l])).reshape(mod_rows, 6, D)

    w_in_b = w_in[l].astype(BF16)
    g1 = row(g_norm1[l])
    q, k, v, u = _input_projection(x, mod3, g1, w_in_b, min(PROJ_TILE, S))
    k_c, v_c = _context_kv(ctx, mod3, g1, w_in_b, ctx_row=B)

    y_na = _attention(q, k, v, k_c, v_c, _bias_classes(rpb[l]), rows_per_tile=ATTN_ROWS_PER_TILE)

    x1 = _mix_out(y_na, u, x, mod3, conv_w[l], row(conv_b[l]), row(ln_g[l]), row(ln_b[l]),
                  w_out[l].astype(BF16), min(MIX_TILE, S))

    return _conv_ffn(x1, mod3, row(g_norm2[l]), w_up[l].astype(BF16), ffn_conv_w[l],
                     row(ffn_conv_b[l]), w_down[l].astype(BF16), row(g_final), min(PROJ_TILE, S))
```

```python
import functools

import numpy as np
import jax
import jax.numpy as jnp
from jax import lax
from jax.experimental import pallas as pl
from jax.experimental.pallas import tpu as pltpu

F32 = jnp.float32
BF16 = jnp.bfloat16

GRID_W = 64
N_HEADS = 8
HEAD_DIM = 64
D_ATTN = N_HEADS * HEAD_DIM
CONV_WIDTH = 31
NA_ROWS = 8
NA_COLS = 16
FFN_CONV_WIDTH = 3
EPS = 1e-6
ATTN_SCALE = HEAD_DIM ** -0.5

HEADS_PER_GROUP = 4
GROUP_W = HEADS_PER_GROUP * HEAD_DIM
N_BIAS_CLASSES = NA_ROWS
CONV_HALO = 16
FFN_HALO = 8
CONV_SPLIT = 16
MIX_ROW_BLOCK = 128
FF_CHUNK = 256
MOD_TILE = 2048
PROJ_TILE = 1024
MIX_TILE = 512
ATTN_ROWS_PER_TILE = 16
ATTN_ROW_UNROLL = 4

VMEM_LIMIT_BYTES = 56 * 1024 * 1024


def _sigmoid(x):
    return 1.0 / (1.0 + jnp.exp(-x))


def _norm_modulate(x, gain, shift, scale):
    ms = jnp.mean(x * x, axis=-1, keepdims=True)
    return (x * lax.rsqrt(ms + EPS)) * (gain * (1.0 + scale)) + shift


def _resident(shape):
    nd = len(shape)
    return pl.BlockSpec(shape, lambda *_: (0,) * nd, pipeline_mode=pl.Buffered(1))


def _mod_kernel(c_ref, w_ref, b_ref, o_ref):
    c = c_ref[...]
    a = (c * _sigmoid(c)).astype(BF16)
    o_ref[...] = jnp.dot(a, w_ref[...].astype(BF16), preferred_element_type=F32) + b_ref[...]


def _modulation(cc, w_mod, b_mod):
    rows, d = cc.shape
    n = w_mod.shape[1]
    tn = MOD_TILE
    return pl.pallas_call(
        _mod_kernel,
        grid=(n // tn,),
        in_specs=[
            pl.BlockSpec((rows, d), lambda j: (0, 0)),
            pl.BlockSpec((d, tn), lambda j: (0, j)),
            pl.BlockSpec((1, tn), lambda j: (0, j)),
        ],
        out_specs=pl.BlockSpec((rows, tn), lambda j: (0, j)),
        out_shape=jax.ShapeDtypeStruct((rows, n), F32),
        compiler_params=pltpu.CompilerParams(
            dimension_semantics=("arbitrary",), vmem_limit_bytes=VMEM_LIMIT_BYTES),
        name="modulation",
    )(cc, w_mod, b_mod)


def _inproj_kernel(x_ref, mod_ref, g_ref, w_ref, q_ref, k_ref, v_ref, u_ref):
    h = _norm_modulate(x_ref[0], g_ref[...], mod_ref[0, 0:1, :], mod_ref[0, 1:2, :]).astype(BF16)

    def proj(lo):
        return jnp.dot(h, w_ref[:, lo:lo + D_ATTN], preferred_element_type=F32)

    q_ref[0] = (proj(0) * ATTN_SCALE).astype(BF16)
    k_ref[0] = proj(D_ATTN).astype(BF16)
    v_ref[0] = proj(2 * D_ATTN).astype(BF16)
    a = proj(3 * D_ATTN)
    g = proj(4 * D_ATTN)
    u_ref[0] = (a * _sigmoid(g)).astype(BF16)


def _ctx_kv_kernel(x_ref, mod_ref, g_ref, w_ref, k_ref, v_ref):
    h = _norm_modulate(x_ref[0], g_ref[...], mod_ref[0, 0:1, :], mod_ref[0, 1:2, :]).astype(BF16)
    k_ref[0] = jnp.dot(h, w_ref[:, D_ATTN:2 * D_ATTN], preferred_element_type=F32).astype(BF16)
    v_ref[0] = jnp.dot(h, w_ref[:, 2 * D_ATTN:3 * D_ATTN], preferred_element_type=F32).astype(BF16)


def _input_projection(x, mod3, g1, w_in, tm):
    B, S, D = x.shape
    out = jax.ShapeDtypeStruct((B, S, D_ATTN), BF16)
    tile = pl.BlockSpec((1, tm, D_ATTN), lambda b, i: (b, i, 0))
    return pl.pallas_call(
        _inproj_kernel,
        grid=(B, S // tm),
        in_specs=[
            pl.BlockSpec((1, tm, D), lambda b, i: (b, i, 0)),
            pl.BlockSpec((1, 6, D), lambda b, i: (b, 0, 0)),
            _resident((1, D)),
            _resident(w_in.shape),
        ],
        out_specs=[tile, tile, tile, tile],
        out_shape=[out, out, out, out],
        compiler_params=pltpu.CompilerParams(
            dimension_semantics=("arbitrary", "arbitrary"), vmem_limit_bytes=VMEM_LIMIT_BYTES),
        name="input_projection",
    )(x, mod3, g1, w_in)


def _context_kv(ctx, mod3, g1, w_in, ctx_row):
    B, L, D = ctx.shape
    out = jax.ShapeDtypeStruct((B, L, D_ATTN), BF16)
    tile = pl.BlockSpec((1, L, D_ATTN), lambda b: (b, 0, 0))
    return pl.pallas_call(
        _ctx_kv_kernel,
        grid=(B,),
        in_specs=[
            pl.BlockSpec((1, L, D), lambda b: (b, 0, 0)),
            pl.BlockSpec((1, 6, D), lambda b: (ctx_row, 0, 0)),
            _resident((1, D)),
            _resident(w_in.shape),
        ],
        out_specs=[tile, tile],
        out_shape=[out, out],
        compiler_params=pltpu.CompilerParams(
            dimension_semantics=("arbitrary",), vmem_limit_bytes=VMEM_LIMIT_BYTES),
        name="context_kv",
    )(ctx, mod3, g1, w_in)


def _bias_classes(rpb):
    H, n_ro, n_co = rpb.shape
    half = n_co // 2
    period = 2 * GRID_W
    ro_pad = n_ro + 1
    z = jnp.concatenate(
        [rpb[..., half:], jnp.zeros((H, n_ro, period - n_co), rpb.dtype), rpb[..., :half]], axis=-1)
    z = jnp.pad(z.astype(F32), ((0, 0), (0, ro_pad - n_ro), (0, 0))).reshape(H * ro_pad, period)
    q = np.arange(GRID_W)
    start = np.clip(q - NA_COLS // 2, 0, GRID_W - NA_COLS)
    kc = np.arange(period) % GRID_W
    valid = (kc[None, :] >= start[:, None]) & (kc[None, :] < start[:, None] + NA_COLS)
    window = jnp.asarray(np.where(valid, 0.0, -np.inf), F32)
    return pl.pallas_call(
        functools.partial(_bias_kernel, n_ro=n_ro),
        grid=(H,),
        in_specs=[
            pl.BlockSpec((ro_pad, period), lambda h: (h, 0)),
            pl.BlockSpec((GRID_W, period), lambda h: (0, 0)),
        ],
        out_specs=pl.BlockSpec((N_BIAS_CLASSES, 1, GRID_W, NA_ROWS * GRID_W), lambda h: (0, h, 0, 0)),
        out_shape=jax.ShapeDtypeStruct((N_BIAS_CLASSES, H, GRID_W, NA_ROWS * GRID_W), F32),
        compiler_params=pltpu.CompilerParams(dimension_semantics=("arbitrary",)),
        name="bias_classes",
    )(z, window)


def _bias_kernel(z_ref, window_ref, o_ref, *, n_ro):
    period = z_ref.shape[-1]
    low_half = lax.broadcasted_iota(jnp.int32, (GRID_W, period), 1) < GRID_W
    toep = [pltpu.roll(jnp.broadcast_to(z_ref[ro:ro + 1, :], (GRID_W, period)), 0, 1, stride=1, stride_axis=0)
            for ro in range(n_ro)]
    pairs = [jnp.where(low_half, toep[ro], pltpu.roll(toep[ro + 1], GRID_W, 1)) + window_ref[...]
             for ro in range(n_ro - 1)]
    for c in range(N_BIAS_CLASSES):
        for jp in range(NA_ROWS // 2):
            o_ref[c, 0, :, jp * period:(jp + 1) * period] = pairs[c + 2 * jp]


def _attn_kernel(q_ref, k_ref, v_ref, kc_ref, vc_ref, bias_ref, o_ref, *, rows_per_tile, n_rows):
    i = pl.program_id(1)
    n_local = NA_ROWS * GRID_W
    lane_head = lax.broadcasted_iota(jnp.int32, (GRID_W, GROUP_W), 1) // HEAD_DIM
    nt = (((1,), (1,)), ((), ()))
    def row_body(r, carry):
        r_abs = i * rows_per_tile + r
        kr0 = jnp.clip(r_abs - NA_ROWS // 2, 0, n_rows - NA_ROWS)
        cls = kr0 - r_abs + (NA_ROWS - 1)
        key0 = pl.multiple_of(kr0 * GRID_W, GRID_W)
        q0 = pl.multiple_of(r * GRID_W, GRID_W)
        for g in range(N_HEADS // HEADS_PER_GROUP):
            ls = slice(g * GROUP_W, (g + 1) * GROUP_W)
            kc = kc_ref[0, :, ls]
            vc = vc_ref[0, :, ls]
            qg = q_ref[0, pl.ds(q0, GRID_W), ls]
            kl = k_ref[0, pl.ds(key0, n_local), ls]
            vl = v_ref[0, pl.ds(key0, n_local), ls]
            qs = jnp.concatenate(
                [jnp.where(lane_head == hh, qg, 0) for hh in range(HEADS_PER_GROUP)], axis=0)
            s_loc = lax.dot_general(qs, kl, nt, preferred_element_type=F32)
            s_ctx = lax.dot_general(qs, kc, nt, preferred_element_type=F32)
            bias = bias_ref[cls, g * HEADS_PER_GROUP:(g + 1) * HEADS_PER_GROUP]
            s_loc = s_loc + bias.reshape(HEADS_PER_GROUP * GRID_W, n_local)
            m = jnp.maximum(jnp.max(s_loc, axis=-1, keepdims=True),
                            jnp.max(s_ctx, axis=-1, keepdims=True))
            p_loc = jnp.exp(s_loc - m)
            p_ctx = jnp.exp(s_ctx - m)
            denom = jnp.sum(p_loc, axis=-1, keepdims=True) + jnp.sum(p_ctx, axis=-1, keepdims=True)
            o = (jnp.dot(p_loc.astype(BF16), vl, preferred_element_type=F32)
                 + jnp.dot(p_ctx.astype(BF16), vc, preferred_element_type=F32))
            o = o * (1.0 / denom)
            y = o[0:GRID_W]
            for hh in range(1, HEADS_PER_GROUP):
                y = jnp.where(lane_head == hh, o[hh * GRID_W:(hh + 1) * GRID_W], y)
            o_ref[0, pl.ds(q0, GRID_W), ls] = y.astype(BF16)
        return carry

    lax.fori_loop(0, rows_per_tile, row_body, 0, unroll=ATTN_ROW_UNROLL)


def _attention(q, k, v, kc, vc, bias, rows_per_tile):
    B, S, _ = q.shape
    L = kc.shape[1]
    n_rows = S // GRID_W
    tm = rows_per_tile * GRID_W
    return pl.pallas_call(
        functools.partial(_attn_kernel, rows_per_tile=rows_per_tile, n_rows=n_rows),
        grid=(B, n_rows // rows_per_tile),
        in_specs=[
            pl.BlockSpec((1, tm, D_ATTN), lambda b, i: (b, i, 0)),
            pl.BlockSpec((1, S, D_ATTN), lambda b, i: (b, 0, 0)),
            pl.BlockSpec((1, S, D_ATTN), lambda b, i: (b, 0, 0)),
            pl.BlockSpec((1, L, D_ATTN), lambda b, i: (b, 0, 0)),
            pl.BlockSpec((1, L, D_ATTN), lambda b, i: (b, 0, 0)),
            _resident(bias.shape),
        ],
        out_specs=pl.BlockSpec((1, tm, D_ATTN), lambda b, i: (b, i, 0)),
        out_shape=jax.ShapeDtypeStruct((B, S, D_ATTN), BF16),
        compiler_params=pltpu.CompilerParams(
            dimension_semantics=("arbitrary", "arbitrary"), vmem_limit_bytes=VMEM_LIMIT_BYTES),
        name="neighbourhood_attention",
    )(q, k, v, kc, vc, bias)


def _shift_sum_matrix():
    span = MIX_ROW_BLOCK + CONV_SPLIT
    s_mat = np.zeros((MIX_ROW_BLOCK, CONV_SPLIT * span), np.float32)
    t = np.arange(MIX_ROW_BLOCK)
    for s in range(CONV_SPLIT):
        s_mat[t, s * span + t + s] = 1.0
    return jnp.asarray(s_mat, BF16)


def _mix_out_kernel(yna_ref, u_ref, up_ref, un_ref, x_ref, mod_ref, cw_ref, cb_ref, lg_ref, lb_ref,
                    shift_ref, w_ref, o_ref, ext_ref, part_ref, *, tm):
    i = pl.program_id(1)
    last = pl.num_programs(1) - 1
    zero = jnp.zeros((CONV_HALO, u_ref.shape[-1]), BF16)
    ext_ref[0:CONV_HALO] = jnp.where(i > 0, up_ref[0], zero)
    ext_ref[CONV_HALO:CONV_HALO + tm] = u_ref[0]
    ext_ref[CONV_HALO + tm:] = jnp.where(i < last, un_ref[0], zero)

    cb = cb_ref[...]
    lg = lg_ref[...]
    lb = lb_ref[...]
    d_attn = yna_ref.shape[-1]
    gate = mod_ref[0, 2:3, :]
    span = MIX_ROW_BLOCK + CONV_SPLIT
    n_tiles = span // CONV_SPLIT
    first_off = CONV_HALO - CONV_WIDTH // 2

    def conv_block(r0, slot):
        win = [ext_ref[pl.ds(r0 + v * CONV_SPLIT, CONV_SPLIT), :] for v in range(n_tiles + 1)]
        for s in range(CONV_SPLIT):
            taps = [(m, CONV_SPLIT * m + s - first_off) for m in range(2)]
            taps = [(m, j) for m, j in taps if 0 <= j < CONV_WIDTH]
            weights = [cw_ref[j] for _, j in taps]
            for v in range(n_tiles):
                part = weights[0] * win[v + taps[0][0]]
                for (m, _), w in zip(taps[1:], weights[1:]):
                    part = part + w * win[v + m]
                part_ref[slot, s * span + v * CONV_SPLIT:s * span + (v + 1) * CONV_SPLIT, :] = part
        acc = jnp.dot(shift_ref[...], part_ref[slot], preferred_element_type=F32) + cb
        mu = jnp.mean(acc, axis=-1, keepdims=True)
        xc = acc - mu
        var = jnp.mean(xc * xc, axis=-1, keepdims=True)
        y = xc * lax.rsqrt(var + EPS) * lg + lb
        return (y * _sigmoid(y)).astype(BF16)

    def block(rb, carry):
        for slot in range(2):
            r0 = pl.multiple_of((2 * rb + slot) * MIX_ROW_BLOCK, MIX_ROW_BLOCK)
            rows = pl.ds(r0, MIX_ROW_BLOCK)
            ycv = conv_block(r0, slot)
            proj = (jnp.dot(yna_ref[0, rows, :], w_ref[0:d_attn, :], preferred_element_type=F32)
                    + jnp.dot(ycv, w_ref[d_attn:, :], preferred_element_type=F32))
            o_ref[0, rows, :] = x_ref[0, rows, :] + gate * proj
        return carry

    lax.fori_loop(0, tm // (2 * MIX_ROW_BLOCK), block, 0)


def _mix_out(y_na, u, x, mod3, conv_w, conv_b, ln_g, ln_b, w_out, tm):
    B, S, D = x.shape
    d_conv = u.shape[-1]
    hb = tm // CONV_HALO
    n_hb = S // CONV_HALO
    cw_rep = jnp.broadcast_to(conv_w.astype(BF16)[:, None, :], (CONV_WIDTH, CONV_SPLIT, d_conv))
    shift = _shift_sum_matrix()
    return pl.pallas_call(
        functools.partial(_mix_out_kernel, tm=tm),
        grid=(B, S // tm),
        in_specs=[
            pl.BlockSpec((1, tm, D_ATTN), lambda b, i: (b, i, 0)),
            pl.BlockSpec((1, tm, d_conv), lambda b, i: (b, i, 0)),
            pl.BlockSpec((1, CONV_HALO, d_conv), lambda b, i: (b, jnp.maximum(i * hb - 1, 0), 0)),
            pl.BlockSpec((1, CONV_HALO, d_conv), lambda b, i: (b, jnp.minimum((i + 1) * hb, n_hb - 1), 0)),
            pl.BlockSpec((1, tm, D), lambda b, i: (b, i, 0)),
            pl.BlockSpec((1, 6, D), lambda b, i: (b, 0, 0)),
            _resident(cw_rep.shape),
            _resident((1, d_conv)),
            _resident((1, d_conv)),
            _resident((1, d_conv)),
            _resident(shift.shape),
            _resident(w_out.shape),
        ],
        out_specs=pl.BlockSpec((1, tm, D), lambda b, i: (b, i, 0)),
        out_shape=jax.ShapeDtypeStruct((B, S, D), F32),
        scratch_shapes=[
            pltpu.VMEM((tm + 2 * CONV_HALO, d_conv), BF16),
            pltpu.VMEM((2, shift.shape[1], d_conv), BF16),
        ],
        compiler_params=pltpu.CompilerParams(
            dimension_semantics=("arbitrary", "arbitrary"), vmem_limit_bytes=VMEM_LIMIT_BYTES),
        name="conv_mixer_out_projection",
    )(y_na, u, u, u, x, mod3, cw_rep, conv_b, ln_g, ln_b, shift, w_out)


def _ffn_kernel(x_ref, xp_ref, xn_ref, mod_ref, g2_ref, wu_ref, fw_ref, fb_ref, wd_ref, gf_ref,
                o_ref, h_ref, act_ref, *, tm, d_ff):
    i = pl.program_id(1)
    last = pl.num_programs(1) - 1
    n_ext = tm + 2 * FFN_HALO
    gain = g2_ref[...]
    shift = mod_ref[0, 3:4, :]
    scale = mod_ref[0, 4:5, :]
    hp = _norm_modulate(xp_ref[0], gain, shift, scale)
    hn = _norm_modulate(xn_ref[0], gain, shift, scale)
    h_ref[0:FFN_HALO] = jnp.where(i > 0, hp, 0.0).astype(BF16)
    h_ref[FFN_HALO:FFN_HALO + tm] = _norm_modulate(x_ref[0], gain, shift, scale).astype(BF16)
    h_ref[FFN_HALO + tm:] = jnp.where(i < last, hn, 0.0).astype(BF16)

    def conv(lo):
        up = jnp.dot(h_ref[...], wu_ref[:, lo:lo + FF_CHUNK], preferred_element_type=F32)
        before = pltpu.roll(up, 1, axis=0)
        after = pltpu.roll(up, n_ext - 1, axis=0)
        mid = slice(FFN_HALO, FFN_HALO + tm)
        w = fw_ref[:, lo:lo + FF_CHUNK]
        return (w[0:1] * before[mid] + w[1:2] * up[mid] + w[2:3] * after[mid]
                + fb_ref[:, lo:lo + FF_CHUNK])

    for f in range(d_ff // FF_CHUNK):
        gate = conv(f * FF_CHUNK)
        val = conv(d_ff + f * FF_CHUNK)
        act_ref[:, f * FF_CHUNK:(f + 1) * FF_CHUNK] = (gate * _sigmoid(gate) * val).astype(BF16)

    y = jnp.dot(act_ref[...], wd_ref[...], preferred_element_type=F32)
    x2 = x_ref[0] + mod_ref[0, 5:6, :] * y
    ms = jnp.mean(x2 * x2, axis=-1, keepdims=True)
    o_ref[0] = x2 * lax.rsqrt(ms + EPS) * gf_ref[...]


def _conv_ffn(x1, mod3, g2, w_up, ffn_w, ffn_b, w_down, g_final, tm):
    B, S, D = x1.shape
    d_ff = w_down.shape[0]
    hb = tm // FFN_HALO
    n_hb = S // FFN_HALO
    return pl.pallas_call(
        functools.partial(_ffn_kernel, tm=tm, d_ff=d_ff),
        grid=(B, S // tm),
        in_specs=[
            pl.BlockSpec((1, tm, D), lambda b, i: (b, i, 0)),
            pl.BlockSpec((1, FFN_HALO, D), lambda b, i: (b, jnp.maximum(i * hb - 1, 0), 0)),
            pl.BlockSpec((1, FFN_HALO, D), lambda b, i: (b, jnp.minimum((i + 1) * hb, n_hb - 1), 0)),
            pl.BlockSpec((1, 6, D), lambda b, i: (b, 0, 0)),
            _resident((1, D)),
            _resident(w_up.shape),
            _resident(ffn_w.shape),
            _resident((1, 2 * d_ff)),
            _resident(w_down.shape),
            _resident((1, D)),
        ],
        out_specs=pl.BlockSpec((1, tm, D), lambda b, i: (b, i, 0)),
        out_shape=jax.ShapeDtypeStruct((B, S, D), F32),
        scratch_shapes=[
            pltpu.VMEM((tm + 2 * FFN_HALO, D), BF16),
            pltpu.VMEM((tm, d_ff), BF16),
        ],
        compiler_params=pltpu.CompilerParams(
            dimension_semantics=("arbitrary", "arbitrary"), vmem_limit_bytes=VMEM_LIMIT_BYTES),
        name="conv_ffn_final_norm",
    )(x1, x1, x1, mod3, g2, w_up, ffn_w, ffn_b, w_down, g_final)


def kernel(x, c, ctx, c_ctx, w_mod, b_mod, g_norm1, w_in, rpb, conv_w, conv_b, ln_g, ln_b, w_out,
           g_norm2, w_up, ffn_conv_w, ffn_conv_b, w_down, g_final):
    B, S, D = x.shape
    depth = w_mod.shape[0]
    assert depth == 1, "the context-stream update of deeper stacks is not implemented"
    assert S % GRID_W == 0 and D_ATTN + conv_w.shape[-1] == D
    l = 0
    row = lambda a: a.reshape(1, -1)

    mod_rows = 8
    cc = jnp.concatenate([c, c_ctx[None], jnp.zeros((mod_rows - B - 1, D), c.dtype)], axis=0)
    mod3 = _modulation(cc, w_mod[l], row(b_mod[l])).reshape(mod_rows, 6, D)

    w_in_b = w_in[l].astype(BF16)
    g1 = row(g_norm1[l])
    q, k, v, u = _input_projection(x, mod3, g1, w_in_b, min(PROJ_TILE, S))
    k_c, v_c = _context_kv(ctx, mod3, g1, w_in_b, ctx_row=B)

    y_na = _attention(q, k, v, k_c, v_c, _bias_classes(rpb[l]), rows_per_tile=ATTN_ROWS_PER_TILE)

    x1 = _mix_out(y_na, u, x, mod3, conv_w[l], row(conv_b[l]), row(ln_g[l]), row(ln_b[l]),
                  w_out[l].astype(BF16), min(MIX_TILE, S))

    return _conv_ffn(x1, mod3, row(g_norm2[l]), w_up[l].astype(BF16), ffn_conv_w[l],
                     row(ffn_conv_b[l]), w_down[l].astype(BF16), row(g_final), min(PROJ_TILE, S))
```

```python
import functools

import numpy as np
import jax
import jax.numpy as jnp
from jax import lax
from jax.experimental import pallas as pl
from jax.experimental.pallas import tpu as pltpu

F32 = jnp.float32
BF16 = jnp.bfloat16

GRID_W = 64
N_HEADS = 8
HEAD_DIM = 64
D_ATTN = N_HEADS * HEAD_DIM
CONV_WIDTH = 31
NA_ROWS = 8
NA_COLS = 16
FFN_CONV_WIDTH = 3
EPS = 1e-6
ATTN_SCALE = HEAD_DIM ** -0.5
LOG2_E = 1.4426950408889634

HEADS_PER_GROUP = 4
GROUP_W = HEADS_PER_GROUP * HEAD_DIM
N_BIAS_CLASSES = NA_ROWS
CONV_HALO = 16
FFN_HALO = 8
CONV_SPLIT = 16
MIX_ROW_BLOCK = 128
MIX_SLOTS = 2
FF_CHUNK = 256
MOD_TILE = 2048
PROJ_TILE = 1024
MIX_TILE = 1024
ATTN_ROWS_PER_TILE = 16
ATTN_ROW_UNROLL = 8

VMEM_LIMIT_BYTES = 56 * 1024 * 1024


def _sigmoid(x):
    return 1.0 / (1.0 + jnp.exp(-x))


def _norm_modulate(x, gain, shift, scale):
    ms = jnp.mean(x * x, axis=-1, keepdims=True)
    return (x * lax.rsqrt(ms + EPS)) * (gain * (1.0 + scale)) + shift


def _resident(shape):
    nd = len(shape)
    return pl.BlockSpec(shape, lambda *_: (0,) * nd, pipeline_mode=pl.Buffered(1))


def _mod_kernel(c_ref, w_ref, b_ref, o_ref):
    c = c_ref[...]
    a = (c * _sigmoid(c)).astype(BF16)
    o_ref[...] = jnp.dot(a, w_ref[...].astype(BF16), preferred_element_type=F32) + b_ref[...]


def _modulation(cc, w_mod, b_mod):
    rows, d = cc.shape
    n = w_mod.shape[1]
    tn = MOD_TILE
    return pl.pallas_call(
        _mod_kernel,
        grid=(n // tn,),
        in_specs=[
            pl.BlockSpec((rows, d), lambda j: (0, 0)),
            pl.BlockSpec((d, tn), lambda j: (0, j)),
            pl.BlockSpec((1, tn), lambda j: (0, j)),
        ],
        out_specs=pl.BlockSpec((rows, tn), lambda j: (0, j)),
        out_shape=jax.ShapeDtypeStruct((rows, n), F32),
        compiler_params=pltpu.CompilerParams(
            dimension_semantics=("arbitrary",), vmem_limit_bytes=VMEM_LIMIT_BYTES),
        name="modulation",
    )(cc, w_mod, b_mod)


def _inproj_kernel(x_ref, mod_ref, g_ref, w_ref, q_ref, k_ref, v_ref, u_ref):
    h = _norm_modulate(x_ref[0], g_ref[...], mod_ref[0, 0:1, :], mod_ref[0, 1:2, :]).astype(BF16)

    def proj(lo):
        return jnp.dot(h, w_ref[:, lo:lo + D_ATTN], preferred_element_type=F32)

    q_ref[0] = (proj(0) * (ATTN_SCALE * LOG2_E)).astype(BF16)
    k_ref[0] = proj(D_ATTN).astype(BF16)
    v_ref[0] = proj(2 * D_ATTN).astype(BF16)
    a = proj(3 * D_ATTN)
    g = proj(4 * D_ATTN)
    u_ref[0] = (a * _sigmoid(g)).astype(BF16)


def _ctx_kv_kernel(x_ref, mod_ref, g_ref, w_ref, k_ref, v_ref):
    h = _norm_modulate(x_ref[0], g_ref[...], mod_ref[0, 0:1, :], mod_ref[0, 1:2, :]).astype(BF16)
    k_ref[0] = jnp.dot(h, w_ref[:, D_ATTN:2 * D_ATTN], preferred_element_type=F32).astype(BF16)
    v_ref[0] = jnp.dot(h, w_ref[:, 2 * D_ATTN:3 * D_ATTN], preferred_element_type=F32).astype(BF16)


def _input_projection(x, mod3, g1, w_in, tm):
    B, S, D = x.shape
    out = jax.ShapeDtypeStruct((B, S, D_ATTN), BF16)
    tile = pl.BlockSpec((1, tm, D_ATTN), lambda b, i: (b, i, 0))
    return pl.pallas_call(
        _inproj_kernel,
        grid=(B, S // tm),
        in_specs=[
            pl.BlockSpec((1, tm, D), lambda b, i: (b, i, 0)),
            pl.BlockSpec((1, 6, D), lambda b, i: (b, 0, 0)),
            _resident((1, D)),
            _resident(w_in.shape),
        ],
        out_specs=[tile, tile, tile, tile],
        out_shape=[out, out, out, out],
        compiler_params=pltpu.CompilerParams(
            dimension_semantics=("arbitrary", "arbitrary"), vmem_limit_bytes=VMEM_LIMIT_BYTES),
        name="input_projection",
    )(x, mod3, g1, w_in)


def _context_kv(ctx, mod3, g1, w_in, ctx_row):
    B, L, D = ctx.shape
    out = jax.ShapeDtypeStruct((B, L, D_ATTN), BF16)
    tile = pl.BlockSpec((1, L, D_ATTN), lambda b: (b, 0, 0))
    return pl.pallas_call(
        _ctx_kv_kernel,
        grid=(B,),
        in_specs=[
            pl.BlockSpec((1, L, D), lambda b: (b, 0, 0)),
            pl.BlockSpec((1, 6, D), lambda b: (ctx_row, 0, 0)),
            _resident((1, D)),
            _resident(w_in.shape),
        ],
        out_specs=[tile, tile],
        out_shape=[out, out],
        compiler_params=pltpu.CompilerParams(
            dimension_semantics=("arbitrary",), vmem_limit_bytes=VMEM_LIMIT_BYTES),
        name="context_kv",
    )(ctx, mod3, g1, w_in)


def _bias_classes(rpb):
    H, n_ro, n_co = rpb.shape
    half = n_co // 2
    period = 2 * GRID_W
    ro_pad = n_ro + 1
    z = jnp.concatenate(
        [rpb[..., half:], jnp.zeros((H, n_ro, period - n_co), rpb.dtype), rpb[..., :half]], axis=-1)
    z = jnp.pad(z.astype(F32), ((0, 0), (0, ro_pad - n_ro), (0, 0))).reshape(H * ro_pad, period)
    q = np.arange(GRID_W)
    start = np.clip(q - NA_COLS // 2, 0, GRID_W - NA_COLS)
    kc = np.arange(period) % GRID_W
    valid = (kc[None, :] >= start[:, None]) & (kc[None, :] < start[:, None] + NA_COLS)
    window = jnp.asarray(np.where(valid, 0.0, -np.inf), F32)
    return pl.pallas_call(
        functools.partial(_bias_kernel, n_ro=n_ro),
        grid=(H,),
        in_specs=[
            pl.BlockSpec((ro_pad, period), lambda h: (h, 0)),
            pl.BlockSpec((GRID_W, period), lambda h: (0, 0)),
        ],
        out_specs=pl.BlockSpec((N_BIAS_CLASSES, 1, GRID_W, NA_ROWS * GRID_W), lambda h: (0, h, 0, 0)),
        out_shape=jax.ShapeDtypeStruct((N_BIAS_CLASSES, H, GRID_W, NA_ROWS * GRID_W), F32),
        compiler_params=pltpu.CompilerParams(dimension_semantics=("arbitrary",)),
        name="bias_classes",
    )(z, window)


def _bias_kernel(z_ref, window_ref, o_ref, *, n_ro):
    period = z_ref.shape[-1]
    low_half = lax.broadcasted_iota(jnp.int32, (GRID_W, period), 1) < GRID_W
    toep = [pltpu.roll(jnp.broadcast_to(z_ref[ro:ro + 1, :], (GRID_W, period)), 0, 1, stride=1, stride_axis=0)
            for ro in range(n_ro)]
    pairs = [jnp.where(low_half, toep[ro], pltpu.roll(toep[ro + 1], GRID_W, 1)) * LOG2_E + window_ref[...]
             for ro in range(n_ro - 1)]
    for c in range(N_BIAS_CLASSES):
        for jp in range(NA_ROWS // 2):
            o_ref[c, 0, :, jp * period:(jp + 1) * period] = pairs[c + 2 * jp]


def _attn_kernel(q_ref, k_ref, v_ref, kc_ref, vc_ref, bias_ref, o_ref, *, rows_per_tile, n_rows):
    i = pl.program_id(1)
    n_local = NA_ROWS * GRID_W
    lane_head = lax.broadcasted_iota(jnp.int32, (GRID_W, GROUP_W), 1) // HEAD_DIM
    nt = (((1,), (1,)), ((), ()))
    def row_body(r, carry):
        r_abs = i * rows_per_tile + r
        kr0 = jnp.clip(r_abs - NA_ROWS // 2, 0, n_rows - NA_ROWS)
        cls = kr0 - r_abs + (NA_ROWS - 1)
        key0 = pl.multiple_of(kr0 * GRID_W, GRID_W)
        q0 = pl.multiple_of(r * GRID_W, GRID_W)
        for g in range(N_HEADS // HEADS_PER_GROUP):
            ls = slice(g * GROUP_W, (g + 1) * GROUP_W)
            kc = kc_ref[0, :, ls]
            vc = vc_ref[0, :, ls]
            qg = q_ref[0, pl.ds(q0, GRID_W), ls]
            kl = k_ref[0, pl.ds(key0, n_local), ls]
            vl = v_ref[0, pl.ds(key0, n_local), ls]
            qs = jnp.concatenate(
                [jnp.where(lane_head == hh, qg, 0) for hh in range(HEADS_PER_GROUP)], axis=0)
            s_loc = lax.dot_general(qs, kl, nt, preferred_element_type=F32)
            s_ctx = lax.dot_general(qs, kc, nt, preferred_element_type=F32)
            bias = bias_ref[cls, g * HEADS_PER_GROUP:(g + 1) * HEADS_PER_GROUP]
            s_loc = s_loc + bias.reshape(HEADS_PER_GROUP * GRID_W, n_local)
            s = jnp.concatenate([s_loc, s_ctx], axis=1)
            p = jnp.exp2(s - jnp.max(s, axis=-1, keepdims=True))
            denom = jnp.sum(p, axis=-1, keepdims=True)
            pb = p.astype(BF16)
            o = (jnp.dot(pb[:, :n_local], vl, preferred_element_type=F32)
                 + jnp.dot(pb[:, n_local:], vc, preferred_element_type=F32))
            o = o * (1.0 / denom)
            y = o[0:GRID_W]
            for hh in range(1, HEADS_PER_GROUP):
                y = jnp.where(lane_head == hh, o[hh * GRID_W:(hh + 1) * GRID_W], y)
            o_ref[0, pl.ds(q0, GRID_W), ls] = y.astype(BF16)
        return carry

    lax.fori_loop(0, rows_per_tile, row_body, 0, unroll=ATTN_ROW_UNROLL)


def _attention(q, k, v, kc, vc, bias, rows_per_tile):
    B, S, _ = q.shape
    L = kc.shape[1]
    n_rows = S // GRID_W
    tm = rows_per_tile * GRID_W
    return pl.pallas_call(
        functools.partial(_attn_kernel, rows_per_tile=rows_per_tile, n_rows=n_rows),
        grid=(B, n_rows // rows_per_tile),
        in_specs=[
            pl.BlockSpec((1, tm, D_ATTN), lambda b, i: (b, i, 0)),
            pl.BlockSpec((1, S, D_ATTN), lambda b, i: (b, 0, 0)),
            pl.BlockSpec((1, S, D_ATTN), lambda b, i: (b, 0, 0)),
            pl.BlockSpec((1, L, D_ATTN), lambda b, i: (b, 0, 0)),
            pl.BlockSpec((1, L, D_ATTN), lambda b, i: (b, 0, 0)),
            _resident(bias.shape),
        ],
        out_specs=pl.BlockSpec((1, tm, D_ATTN), lambda b, i: (b, i, 0)),
        out_shape=jax.ShapeDtypeStruct((B, S, D_ATTN), BF16),
        compiler_params=pltpu.CompilerParams(
            dimension_semantics=("arbitrary", "arbitrary"), vmem_limit_bytes=VMEM_LIMIT_BYTES),
        name="neighbourhood_attention",
    )(q, k, v, kc, vc, bias)


def _shift_sum_matrix():
    span = MIX_ROW_BLOCK + CONV_SPLIT
    s_mat = np.zeros((MIX_ROW_BLOCK, CONV_SPLIT * span), np.float32)
    t = np.arange(MIX_ROW_BLOCK)
    for s in range(CONV_SPLIT):
        s_mat[t, s * span + t + s] = 1.0
    return jnp.asarray(s_mat, BF16)


def _mix_out_kernel(yna_ref, u_ref, up_ref, un_ref, x_ref, mod_ref, cw_ref, cb_ref, lg_ref, lb_ref,
                    shift_ref, w_ref, o_ref, ext_ref, part_ref, ycv_ref, *, tm):
    i = pl.program_id(1)
    last = pl.num_programs(1) - 1
    zero = jnp.zeros((CONV_HALO, u_ref.shape[-1]), BF16)
    ext_ref[0:CONV_HALO] = jnp.where(i > 0, up_ref[0], zero)
    ext_ref[CONV_HALO:CONV_HALO + tm] = u_ref[0]
    ext_ref[CONV_HALO + tm:] = jnp.where(i < last, un_ref[0], zero)

    cb = cb_ref[...]
    lg = lg_ref[...]
    lb = lb_ref[...]
    d_attn = yna_ref.shape[-1]
    gate = mod_ref[0, 2:3, :]
    span = MIX_ROW_BLOCK + CONV_SPLIT
    n_tiles = span // CONV_SPLIT
    first_off = CONV_HALO - CONV_WIDTH // 2

    def partials(blk, slot):
        r0 = pl.multiple_of(blk * MIX_ROW_BLOCK, MIX_ROW_BLOCK)
        win = [ext_ref[pl.ds(r0 + v * CONV_SPLIT, CONV_SPLIT), :] for v in range(n_tiles + 1)]
        for s in range(CONV_SPLIT):
            taps = [(m, CONV_SPLIT * m + s - first_off) for m in range(2)]
            taps = [(m, j) for m, j in taps if 0 <= j < CONV_WIDTH]
            weights = [cw_ref[j] for _, j in taps]
            for v in range(n_tiles):
                part = weights[0] * win[v + taps[0][0]]
                for (m, _), w in zip(taps[1:], weights[1:]):
                    part = part + w * win[v + m]
                part_ref[slot, s * span + v * CONV_SPLIT:s * span + (v + 1) * CONV_SPLIT, :] = part

    def conv_norm(slot):
        acc = jnp.dot(shift_ref[...], part_ref[slot], preferred_element_type=F32) + cb
        mu = jnp.mean(acc, axis=-1, keepdims=True)
        xc = acc - mu
        var = jnp.mean(xc * xc, axis=-1, keepdims=True)
        y = xc * lax.rsqrt(var + EPS) * lg + lb
        ycv_ref[slot] = (y * _sigmoid(y)).astype(BF16)

    def project(blk, slot):
        rows = pl.ds(pl.multiple_of(blk * MIX_ROW_BLOCK, MIX_ROW_BLOCK), MIX_ROW_BLOCK)
        proj = (jnp.dot(yna_ref[0, rows, :], w_ref[0:d_attn, :], preferred_element_type=F32)
                + jnp.dot(ycv_ref[slot], w_ref[d_attn:, :], preferred_element_type=F32))
        o_ref[0, rows, :] = x_ref[0, rows, :] + gate * proj

    n_blocks = tm // MIX_ROW_BLOCK
    assert n_blocks >= 3
    partials(0, 0)
    partials(1, 1)
    conv_norm(0)
    conv_norm(1)
    partials(2, 0)
    for b in range(n_blocks):
        project(b, b % 2)
        if b + 2 < n_blocks:
            conv_norm(b % 2)
        if b + 3 < n_blocks:
            partials(b + 3, (b + 1) % 2)


def _mix_out(y_na, u, x, mod3, conv_w, conv_b, ln_g, ln_b, w_out, tm):
    B, S, D = x.shape
    d_conv = u.shape[-1]
    hb = tm // CONV_HALO
    n_hb = S // CONV_HALO
    cw_rep = jnp.broadcast_to(conv_w.astype(BF16)[:, None, :], (CONV_WIDTH, CONV_SPLIT, d_conv))
    shift = _shift_sum_matrix()
    return pl.pallas_call(
        functools.partial(_mix_out_kernel, tm=tm),
        grid=(B, S // tm),
        in_specs=[
            pl.BlockSpec((1, tm, D_ATTN), lambda b, i: (b, i, 0)),
            pl.BlockSpec((1, tm, d_conv), lambda b, i: (b, i, 0)),
            pl.BlockSpec((1, CONV_HALO, d_conv), lambda b, i: (b, jnp.maximum(i * hb - 1, 0), 0)),
            pl.BlockSpec((1, CONV_HALO, d_conv), lambda b, i: (b, jnp.minimum((i + 1) * hb, n_hb - 1), 0)),
            pl.BlockSpec((1, tm, D), lambda b, i: (b, i, 0)),
            pl.BlockSpec((1, 6, D), lambda b, i: (b, 0, 0)),
            _resident(cw_rep.shape),
            _resident((1, d_conv)),
            _resident((1, d_conv)),
            _resident((1, d_conv)),
            _resident(shift.shape),
            _resident(w_out.shape),
        ],
        out_specs=pl.BlockSpec((1, tm, D), lambda b, i: (b, i, 0)),
        out_shape=jax.ShapeDtypeStruct((B, S, D), F32),
        scratch_shapes=[
            pltpu.VMEM((tm + 2 * CONV_HALO, d_conv), BF16),
            pltpu.VMEM((MIX_SLOTS, shift.shape[1], d_conv), BF16),
            pltpu.VMEM((MIX_SLOTS, MIX_ROW_BLOCK, d_conv), BF16),
        ],
        compiler_params=pltpu.CompilerParams(
            dimension_semantics=("arbitrary", "arbitrary"), vmem_limit_bytes=VMEM_LIMIT_BYTES),
        name="conv_mixer_out_projection",
    )(y_na, u, u, u, x, mod3, cw_rep, conv_b, ln_g, ln_b, shift, w_out)


def _ffn_kernel(x_ref, xp_ref, xn_ref, mod_ref, g2_ref, wu_ref, fw_ref, fb_ref, wd_ref, gf_ref,
                o_ref, h_ref, act_ref, *, tm, d_ff):
    i = pl.program_id(1)
    last = pl.num_programs(1) - 1
    n_ext = tm + 2 * FFN_HALO
    gain = g2_ref[...]
    shift = mod_ref[0, 3:4, :]
    scale = mod_ref[0, 4:5, :]
    hp = _norm_modulate(xp_ref[0], gain, shift, scale)
    hn = _norm_modulate(xn_ref[0], gain, shift, scale)
    h_ref[0:FFN_HALO] = jnp.where(i > 0, hp, 0.0).astype(BF16)
    h_ref[FFN_HALO:FFN_HALO + tm] = _norm_modulate(x_ref[0], gain, shift, scale).astype(BF16)
    h_ref[FFN_HALO + tm:] = jnp.where(i < last, hn, 0.0).astype(BF16)

    def conv(lo):
        up = jnp.dot(h_ref[...], wu_ref[:, lo:lo + FF_CHUNK], preferred_element_type=F32)
        before = pltpu.roll(up, 1, axis=0)
        after = pltpu.roll(up, n_ext - 1, axis=0)
        mid = slice(FFN_HALO, FFN_HALO + tm)
        w = fw_ref[:, lo:lo + FF_CHUNK]
        return (w[0:1] * before[mid] + w[1:2] * up[mid] + w[2:3] * after[mid]
                + fb_ref[:, lo:lo + FF_CHUNK])

    for f in range(d_ff // FF_CHUNK):
        gate = conv(f * FF_CHUNK)
        val = conv(d_ff + f * FF_CHUNK)
        act_ref[:, f * FF_CHUNK:(f + 1) * FF_CHUNK] = (gate * _sigmoid(gate) * val).astype(BF16)

    y = jnp.dot(act_ref[...], wd_ref[...], preferred_element_type=F32)
    x2 = x_ref[0] + mod_ref[0, 5:6, :] * y
    ms = jnp.mean(x2 * x2, axis=-1, keepdims=True)
    o_ref[0] = x2 * lax.rsqrt(ms + EPS) * gf_ref[...]


def _conv_ffn(x1, mod3, g2, w_up, ffn_w, ffn_b, w_down, g_final, tm):
    B, S, D = x1.shape
    d_ff = w_down.shape[0]
    hb = tm // FFN_HALO
    n_hb = S // FFN_HALO
    return pl.pallas_call(
        functools.partial(_ffn_kernel, tm=tm, d_ff=d_ff),
        grid=(B, S // tm),
        in_specs=[
            pl.BlockSpec((1, tm, D), lambda b, i: (b, i, 0)),
            pl.BlockSpec((1, FFN_HALO, D), lambda b, i: (b, jnp.maximum(i * hb - 1, 0), 0)),
            pl.BlockSpec((1, FFN_HALO, D), lambda b, i: (b, jnp.minimum((i + 1) * hb, n_hb - 1), 0)),
            pl.BlockSpec((1, 6, D), lambda b, i: (b, 0, 0)),
            _resident((1, D)),
            _resident(w_up.shape),
            _resident(ffn_w.shape),
            _resident((1, 2 * d_ff)),
            _resident(w_down.shape),
            _resident((1, D)),
        ],
        out_specs=pl.BlockSpec((1, tm, D), lambda b, i: (b, i, 0)),
        out_shape=jax.ShapeDtypeStruct((B, S, D), F32),
        scratch_shapes=[
            pltpu.VMEM((tm + 2 * FFN_HALO, D), BF16),
            pltpu.VMEM((tm, d_ff), BF16),
        ],
        compiler_params=pltpu.CompilerParams(
            dimension_semantics=("arbitrary", "arbitrary"), vmem_limit_bytes=VMEM_LIMIT_BYTES),
        name="conv_ffn_final_norm",
    )(x1, x1, x1, mod3, g2, w_up, ffn_w, ffn_b, w_down, g_final)


def kernel(x, c, ctx, c_ctx, w_mod, b_mod, g_norm1, w_in, rpb, conv_w, conv_b, ln_g, ln_b, w_out,
           g_norm2, w_up, ffn_conv_w, ffn_conv_b, w_down, g_final):
    B, S, D = x.shape
    depth = w_mod.shape[0]
    assert depth == 1, "the context-stream update of deeper stacks is not implemented"
    assert S % GRID_W == 0 and D_ATTN + conv_w.shape[-1] == D
    l = 0
    row = lambda a: a.reshape(1, -1)

    mod_rows = 8
    cc = jnp.concatenate([c, c_ctx[None], jnp.zeros((mod_rows - B - 1, D), c.dtype)], axis=0)
    mod3 = _modulation(cc, w_mod[l], row(b_mod[l])).reshape(mod_rows, 6, D)

    w_in_b = w_in[l].astype(BF16)
    g1 = row(g_norm1[l])
    q, k, v, u = _input_projection(x, mod3, g1, w_in_b, min(PROJ_TILE, S))
    k_c, v_c = _context_kv(ctx, mod3, g1, w_in_b, ctx_row=B)

    y_na = _attention(q, k, v, k_c, v_c, _bias_classes(rpb[l]), rows_per_tile=ATTN_ROWS_PER_TILE)

    x1 = _mix_out(y_na, u, x, mod3, conv_w[l], row(conv_b[l]), row(ln_g[l]), row(ln_b[l]),
                  w_out[l].astype(BF16), min(MIX_TILE, S))

    return _conv_ffn(x1, mod3, row(g_norm2[l]), w_up[l].astype(BF16), ffn_conv_w[l],
                     row(ffn_conv_b[l]), w_down[l].astype(BF16), row(g_final), min(PROJ_TILE, S))
```

```python
import functools

import numpy as np
import jax
import jax.numpy as jnp
from jax import lax
from jax.experimental import pallas as pl
from jax.experimental.pallas import tpu as pltpu

F32 = jnp.float32
BF16 = jnp.bfloat16

GRID_W = 64
N_HEADS = 8
HEAD_DIM = 64
D_ATTN = N_HEADS * HEAD_DIM
CONV_WIDTH = 31
NA_ROWS = 8
NA_COLS = 16
FFN_CONV_WIDTH = 3
EPS = 1e-6
ATTN_SCALE = HEAD_DIM ** -0.5
LOG2_E = 1.4426950408889634

HEADS_PER_GROUP = 4
GROUP_W = HEADS_PER_GROUP * HEAD_DIM
N_BIAS_CLASSES = NA_ROWS
CONV_HALO = 16
FFN_HALO = 8
CONV_SPLIT = 16
MIX_ROW_BLOCK = 128
MIX_SLOTS = 2
FF_CHUNK = 256
MOD_TILE = 2048
PROJ_TILE = 1024
MIX_TILE = 1024
ATTN_ROWS_PER_TILE = 16
ATTN_ROW_UNROLL = 8

VMEM_LIMIT_BYTES = 56 * 1024 * 1024


def _sigmoid(x):
    return 1.0 / (1.0 + jnp.exp(-x))


def _norm_modulate(x, gain, shift, scale):
    ms = jnp.mean(x * x, axis=-1, keepdims=True)
    return (x * lax.rsqrt(ms + EPS)) * (gain * (1.0 + scale)) + shift


def _resident(shape):
    nd = len(shape)
    return pl.BlockSpec(shape, lambda *_: (0,) * nd, pipeline_mode=pl.Buffered(1))


def _mod_kernel(c_ref, w_ref, b_ref, o_ref):
    c = c_ref[...]
    a = (c * _sigmoid(c)).astype(BF16)
    o_ref[...] = jnp.dot(a, w_ref[...].astype(BF16), preferred_element_type=F32) + b_ref[...]


def _modulation(cc, w_mod, b_mod):
    rows, d = cc.shape
    n = w_mod.shape[1]
    tn = MOD_TILE
    return pl.pallas_call(
        _mod_kernel,
        grid=(n // tn,),
        in_specs=[
            pl.BlockSpec((rows, d), lambda j: (0, 0)),
            pl.BlockSpec((d, tn), lambda j: (0, j)),
            pl.BlockSpec((1, tn), lambda j: (0, j)),
        ],
        out_specs=pl.BlockSpec((rows, tn), lambda j: (0, j)),
        out_shape=jax.ShapeDtypeStruct((rows, n), F32),
        compiler_params=pltpu.CompilerParams(
            dimension_semantics=("arbitrary",), vmem_limit_bytes=VMEM_LIMIT_BYTES),
        name="modulation",
    )(cc, w_mod, b_mod)


def _inproj_kernel(x_ref, mod_ref, g_ref, w_ref, q_ref, k_ref, v_ref, u_ref):
    h = _norm_modulate(x_ref[0], g_ref[...], mod_ref[0, 0:1, :], mod_ref[0, 1:2, :]).astype(BF16)

    def proj(lo):
        return jnp.dot(h, w_ref[:, lo:lo + D_ATTN].astype(BF16), preferred_element_type=F32)

    q_ref[0] = (proj(0) * (ATTN_SCALE * LOG2_E)).astype(BF16)
    k_ref[0] = proj(D_ATTN).astype(BF16)
    v_ref[0] = proj(2 * D_ATTN).astype(BF16)
    a = proj(3 * D_ATTN)
    g = proj(4 * D_ATTN)
    u_ref[0] = (a * _sigmoid(g)).astype(BF16)


def _ctx_kv_kernel(x_ref, mod_ref, g_ref, w_ref, k_ref, v_ref):
    h = _norm_modulate(x_ref[0], g_ref[...], mod_ref[0, 0:1, :], mod_ref[0, 1:2, :]).astype(BF16)
    wk = w_ref[:, D_ATTN:2 * D_ATTN].astype(BF16)
    wv = w_ref[:, 2 * D_ATTN:3 * D_ATTN].astype(BF16)
    k_ref[0] = jnp.dot(h, wk, preferred_element_type=F32).astype(BF16)
    v_ref[0] = jnp.dot(h, wv, preferred_element_type=F32).astype(BF16)


def _input_projection(x, mod3, g1, w_in, tm):
    B, S, D = x.shape
    out = jax.ShapeDtypeStruct((B, S, D_ATTN), BF16)
    tile = pl.BlockSpec((1, tm, D_ATTN), lambda b, i: (b, i, 0))
    return pl.pallas_call(
        _inproj_kernel,
        grid=(B, S // tm),
        in_specs=[
            pl.BlockSpec((1, tm, D), lambda b, i: (b, i, 0)),
            pl.BlockSpec((1, 6, D), lambda b, i: (b, 0, 0)),
            _resident((1, D)),
            _resident(w_in.shape),
        ],
        out_specs=[tile, tile, tile, tile],
        out_shape=[out, out, out, out],
        compiler_params=pltpu.CompilerParams(
            dimension_semantics=("arbitrary", "arbitrary"), vmem_limit_bytes=VMEM_LIMIT_BYTES),
        name="input_projection",
    )(x, mod3, g1, w_in)


def _context_kv(ctx, mod3, g1, w_in, ctx_row):
    B, L, D = ctx.shape
    out = jax.ShapeDtypeStruct((B, L, D_ATTN), BF16)
    tile = pl.BlockSpec((1, L, D_ATTN), lambda b: (b, 0, 0))
    return pl.pallas_call(
        _ctx_kv_kernel,
        grid=(B,),
        in_specs=[
            pl.BlockSpec((1, L, D), lambda b: (b, 0, 0)),
            pl.BlockSpec((1, 6, D), lambda b: (ctx_row, 0, 0)),
            _resident((1, D)),
            _resident(w_in.shape),
        ],
        out_specs=[tile, tile],
        out_shape=[out, out],
        compiler_params=pltpu.CompilerParams(
            dimension_semantics=("arbitrary",), vmem_limit_bytes=VMEM_LIMIT_BYTES),
        name="context_kv",
    )(ctx, mod3, g1, w_in)


def _bias_classes(rpb):
    H, n_ro, n_co = rpb.shape
    half = n_co // 2
    period = 2 * GRID_W
    ro_pad = n_ro + 1
    z = jnp.concatenate(
        [rpb[..., half:], jnp.zeros((H, n_ro, period - n_co), rpb.dtype), rpb[..., :half]], axis=-1)
    z = jnp.pad(z.astype(F32), ((0, 0), (0, ro_pad - n_ro), (0, 0))).reshape(H * ro_pad, period)
    q = np.arange(GRID_W)
    start = np.clip(q - NA_COLS // 2, 0, GRID_W - NA_COLS)
    kc = np.arange(period) % GRID_W
    valid = (kc[None, :] >= start[:, None]) & (kc[None, :] < start[:, None] + NA_COLS)
    window = jnp.asarray(np.where(valid, 0.0, -np.inf), F32)
    return pl.pallas_call(
        functools.partial(_bias_kernel, n_ro=n_ro),
        grid=(H,),
        in_specs=[
            pl.BlockSpec((ro_pad, period), lambda h: (h, 0)),
            pl.BlockSpec((GRID_W, period), lambda h: (0, 0)),
        ],
        out_specs=pl.BlockSpec((N_BIAS_CLASSES, 1, GRID_W, NA_ROWS * GRID_W), lambda h: (0, h, 0, 0)),
        out_shape=jax.ShapeDtypeStruct((N_BIAS_CLASSES, H, GRID_W, NA_ROWS * GRID_W), F32),
        compiler_params=pltpu.CompilerParams(dimension_semantics=("arbitrary",)),
        name="bias_classes",
    )(z, window)


def _bias_kernel(z_ref, window_ref, o_ref, *, n_ro):
    period = z_ref.shape[-1]
    low_half = lax.broadcasted_iota(jnp.int32, (GRID_W, period), 1) < GRID_W
    toep = [pltpu.roll(jnp.broadcast_to(z_ref[ro:ro + 1, :], (GRID_W, period)), 0, 1, stride=1, stride_axis=0)
            for ro in range(n_ro)]
    pairs = [jnp.where(low_half, toep[ro], pltpu.roll(toep[ro + 1], GRID_W, 1)) * LOG2_E + window_ref[...]
             for ro in range(n_ro - 1)]
    for c in range(N_BIAS_CLASSES):
        for jp in range(NA_ROWS // 2):
            o_ref[c, 0, :, jp * period:(jp + 1) * period] = pairs[c + 2 * jp]


def _attn_kernel(q_ref, k_ref, v_ref, kc_ref, vc_ref, bias_ref, o_ref, *, rows_per_tile, n_rows):
    i = pl.program_id(1)
    n_local = NA_ROWS * GRID_W
    lane_head = lax.broadcasted_iota(jnp.int32, (GRID_W, GROUP_W), 1) // HEAD_DIM
    nt = (((1,), (1,)), ((), ()))
    def row_body(r, carry):
        r_abs = i * rows_per_tile + r
        kr0 = jnp.clip(r_abs - NA_ROWS // 2, 0, n_rows - NA_ROWS)
        cls = kr0 - r_abs + (NA_ROWS - 1)
        key0 = pl.multiple_of(kr0 * GRID_W, GRID_W)
        q0 = pl.multiple_of(r * GRID_W, GRID_W)
        for g in range(N_HEADS // HEADS_PER_GROUP):
            ls = slice(g * GROUP_W, (g + 1) * GROUP_W)
            kc = kc_ref[0, :, ls]
            vc = vc_ref[0, :, ls]
            qg = q_ref[0, pl.ds(q0, GRID_W), ls]
            kl = k_ref[0, pl.ds(key0, n_local), ls]
            vl = v_ref[0, pl.ds(key0, n_local), ls]
            qs = jnp.concatenate(
                [jnp.where(lane_head == hh, qg, 0) for hh in range(HEADS_PER_GROUP)], axis=0)
            s_loc = lax.dot_general(qs, kl, nt, preferred_element_type=F32)
            s_ctx = lax.dot_general(qs, kc, nt, preferred_element_type=F32)
            bias = bias_ref[cls, g * HEADS_PER_GROUP:(g + 1) * HEADS_PER_GROUP]
            s_loc = s_loc + bias.reshape(HEADS_PER_GROUP * GRID_W, n_local)
            s = jnp.concatenate([s_loc, s_ctx], axis=1)
            p = jnp.exp2(s - jnp.max(s, axis=-1, keepdims=True))
            denom = jnp.sum(p, axis=-1, keepdims=True)
            pb = p.astype(BF16)
            o = (jnp.dot(pb[:, :n_local], vl, preferred_element_type=F32)
                 + jnp.dot(pb[:, n_local:], vc, preferred_element_type=F32))
            o = o * (1.0 / denom)
            y = o[0:GRID_W]
            for hh in range(1, HEADS_PER_GROUP):
                y = jnp.where(lane_head == hh, o[hh * GRID_W:(hh + 1) * GRID_W], y)
            o_ref[0, pl.ds(q0, GRID_W), ls] = y.astype(BF16)
        return carry

    lax.fori_loop(0, rows_per_tile, row_body, 0, unroll=ATTN_ROW_UNROLL)


def _attention(q, k, v, kc, vc, bias, rows_per_tile):
    B, S, _ = q.shape
    L = kc.shape[1]
    n_rows = S // GRID_W
    tm = rows_per_tile * GRID_W
    return pl.pallas_call(
        functools.partial(_attn_kernel, rows_per_tile=rows_per_tile, n_rows=n_rows),
        grid=(B, n_rows // rows_per_tile),
        in_specs=[
            pl.BlockSpec((1, tm, D_ATTN), lambda b, i: (b, i, 0)),
            pl.BlockSpec((1, S, D_ATTN), lambda b, i: (b, 0, 0)),
            pl.BlockSpec((1, S, D_ATTN), lambda b, i: (b, 0, 0)),
            pl.BlockSpec((1, L, D_ATTN), lambda b, i: (b, 0, 0)),
            pl.BlockSpec((1, L, D_ATTN), lambda b, i: (b, 0, 0)),
            _resident(bias.shape),
        ],
        out_specs=pl.BlockSpec((1, tm, D_ATTN), lambda b, i: (b, i, 0)),
        out_shape=jax.ShapeDtypeStruct((B, S, D_ATTN), BF16),
        compiler_params=pltpu.CompilerParams(
            dimension_semantics=("arbitrary", "arbitrary"), vmem_limit_bytes=VMEM_LIMIT_BYTES),
        name="neighbourhood_attention",
    )(q, k, v, kc, vc, bias)


def _shift_sum_matrix():
    span = MIX_ROW_BLOCK + CONV_SPLIT
    s_mat = np.zeros((MIX_ROW_BLOCK, CONV_SPLIT * span), np.float32)
    t = np.arange(MIX_ROW_BLOCK)
    for s in range(CONV_SPLIT):
        s_mat[t, s * span + t + s] = 1.0
    return jnp.asarray(s_mat, BF16)


def _mix_out_kernel(yna_ref, u_ref, up_ref, un_ref, x_ref, mod_ref, cw_ref, cb_ref, lg_ref, lb_ref,
                    shift_ref, w_ref, wu_ref, wd_ref, o_ref, wub_ref, wdb_ref, ext_ref, part_ref, ycv_ref,
                    *, tm):
    wub_ref[...] = wu_ref[...].astype(BF16)
    wdb_ref[...] = wd_ref[...].astype(BF16)
    i = pl.program_id(1)
    last = pl.num_programs(1) - 1
    zero = jnp.zeros((CONV_HALO, u_ref.shape[-1]), BF16)
    ext_ref[0:CONV_HALO] = jnp.where(i > 0, up_ref[0], zero)
    ext_ref[CONV_HALO:CONV_HALO + tm] = u_ref[0]
    ext_ref[CONV_HALO + tm:] = jnp.where(i < last, un_ref[0], zero)

    cb = cb_ref[...]
    lg = lg_ref[...]
    lb = lb_ref[...]
    d_attn = yna_ref.shape[-1]
    gate = mod_ref[0, 2:3, :]
    span = MIX_ROW_BLOCK + CONV_SPLIT
    n_tiles = span // CONV_SPLIT
    first_off = CONV_HALO - CONV_WIDTH // 2

    def partials(blk, slot):
        r0 = pl.multiple_of(blk * MIX_ROW_BLOCK, MIX_ROW_BLOCK)
        win = [ext_ref[pl.ds(r0 + v * CONV_SPLIT, CONV_SPLIT), :] for v in range(n_tiles + 1)]
        for s in range(CONV_SPLIT):
            taps = [(m, CONV_SPLIT * m + s - first_off) for m in range(2)]
            taps = [(m, j) for m, j in taps if 0 <= j < CONV_WIDTH]
            weights = [cw_ref[j] for _, j in taps]
            for v in range(n_tiles):
                part = weights[0] * win[v + taps[0][0]]
                for (m, _), w in zip(taps[1:], weights[1:]):
                    part = part + w * win[v + m]
                part_ref[slot, s * span + v * CONV_SPLIT:s * span + (v + 1) * CONV_SPLIT, :] = part

    def conv_norm(slot):
        acc = jnp.dot(shift_ref[...], part_ref[slot], preferred_element_type=F32) + cb
        mu = jnp.mean(acc, axis=-1, keepdims=True)
        xc = acc - mu
        var = jnp.mean(xc * xc, axis=-1, keepdims=True)
        y = xc * lax.rsqrt(var + EPS) * lg + lb
        ycv_ref[slot] = (y * _sigmoid(y)).astype(BF16)

    def project(blk, slot):
        rows = pl.ds(pl.multiple_of(blk * MIX_ROW_BLOCK, MIX_ROW_BLOCK), MIX_ROW_BLOCK)
        proj = (jnp.dot(yna_ref[0, rows, :], w_ref[0:d_attn, :], preferred_element_type=F32)
                + jnp.dot(ycv_ref[slot], w_ref[d_attn:, :], preferred_element_type=F32))
        o_ref[0, rows, :] = x_ref[0, rows, :] + gate * proj

    n_blocks = tm // MIX_ROW_BLOCK
    assert n_blocks >= 3
    partials(0, 0)
    partials(1, 1)
    conv_norm(0)
    conv_norm(1)
    partials(2, 0)
    for b in range(n_blocks):
        project(b, b % 2)
        if b + 2 < n_blocks:
            conv_norm(b % 2)
        if b + 3 < n_blocks:
            partials(b + 3, (b + 1) % 2)


def _mix_out(y_na, u, x, mod3, conv_w, conv_b, ln_g, ln_b, w_out, w_up, w_down, tm):
    B, S, D = x.shape
    d_conv = u.shape[-1]
    hb = tm // CONV_HALO
    n_hb = S // CONV_HALO
    cw_rep = jnp.broadcast_to(conv_w.astype(BF16)[:, None, :], (CONV_WIDTH, CONV_SPLIT, d_conv))
    shift = _shift_sum_matrix()
    n_steps = S // tm
    d_ff = w_down.shape[0]
    n_slabs = d_ff // FF_CHUNK
    assert B * n_steps >= n_slabs and w_up.shape[1] == 2 * d_ff
    slab = lambda b, i: jnp.minimum(b * n_steps + i, n_slabs - 1)
    return pl.pallas_call(
        functools.partial(_mix_out_kernel, tm=tm),
        grid=(B, S // tm),
        in_specs=[
            pl.BlockSpec((1, tm, D_ATTN), lambda b, i: (b, i, 0)),
            pl.BlockSpec((1, tm, d_conv), lambda b, i: (b, i, 0)),
            pl.BlockSpec((1, CONV_HALO, d_conv), lambda b, i: (b, jnp.maximum(i * hb - 1, 0), 0)),
            pl.BlockSpec((1, CONV_HALO, d_conv), lambda b, i: (b, jnp.minimum((i + 1) * hb, n_hb - 1), 0)),
            pl.BlockSpec((1, tm, D), lambda b, i: (b, i, 0)),
            pl.BlockSpec((1, 6, D), lambda b, i: (b, 0, 0)),
            _resident(cw_rep.shape),
            _resident((1, d_conv)),
            _resident((1, d_conv)),
            _resident((1, d_conv)),
            _resident(shift.shape),
            _resident(w_out.shape),
            pl.BlockSpec((D, 2 * FF_CHUNK), lambda b, i: (0, slab(b, i))),
            pl.BlockSpec((FF_CHUNK, D), lambda b, i: (slab(b, i), 0)),
        ],
        out_specs=[
            pl.BlockSpec((1, tm, D), lambda b, i: (b, i, 0)),
            pl.BlockSpec((D, 2 * FF_CHUNK), lambda b, i: (0, slab(b, i))),
            pl.BlockSpec((FF_CHUNK, D), lambda b, i: (slab(b, i), 0)),
        ],
        out_shape=[
            jax.ShapeDtypeStruct((B, S, D), F32),
            jax.ShapeDtypeStruct(w_up.shape, BF16),
            jax.ShapeDtypeStruct(w_down.shape, BF16),
        ],
        scratch_shapes=[
            pltpu.VMEM((tm + 2 * CONV_HALO, d_conv), BF16),
            pltpu.VMEM((MIX_SLOTS, shift.shape[1], d_conv), BF16),
            pltpu.VMEM((MIX_SLOTS, MIX_ROW_BLOCK, d_conv), BF16),
        ],
        compiler_params=pltpu.CompilerParams(
            dimension_semantics=("arbitrary", "arbitrary"), vmem_limit_bytes=VMEM_LIMIT_BYTES),
        name="conv_mixer_out_projection",
    )(y_na, u, u, u, x, mod3, cw_rep, conv_b, ln_g, ln_b, shift, w_out, w_up, w_down)


def _ffn_kernel(x_ref, xp_ref, xn_ref, mod_ref, g2_ref, wu_ref, fw_ref, fb_ref, wd_ref, gf_ref,
                o_ref, h_ref, act_ref, *, tm, d_ff):
    i = pl.program_id(1)
    last = pl.num_programs(1) - 1
    gain = g2_ref[...]
    shift = mod_ref[0, 3:4, :]
    scale = mod_ref[0, 4:5, :]
    hp = _norm_modulate(xp_ref[0], gain, shift, scale)
    hn = _norm_modulate(xn_ref[0], gain, shift, scale)
    h_ref[0:FFN_HALO] = jnp.where(i > 0, hp, 0.0).astype(BF16)
    h_ref[FFN_HALO:FFN_HALO + tm] = _norm_modulate(x_ref[0], gain, shift, scale).astype(BF16)
    h_ref[FFN_HALO + tm:] = jnp.where(i < last, hn, 0.0).astype(BF16)

    n_ext = tm + 2 * FFN_HALO

    def conv(lo):
        up = jnp.dot(h_ref[...], wu_ref[:, lo:lo + FF_CHUNK], preferred_element_type=F32)
        before = pltpu.roll(up, 1, axis=0)
        after = pltpu.roll(up, n_ext - 1, axis=0)
        mid = slice(FFN_HALO, FFN_HALO + tm)
        w = fw_ref[:, lo:lo + FF_CHUNK]
        return (w[0:1] * before[mid] + w[1:2] * up[mid] + w[2:3] * after[mid]
                + fb_ref[:, lo:lo + FF_CHUNK])

    for f in range(d_ff // FF_CHUNK):
        gate = conv(f * FF_CHUNK)
        val = conv(d_ff + f * FF_CHUNK)
        act_ref[:, f * FF_CHUNK:(f + 1) * FF_CHUNK] = (gate * _sigmoid(gate) * val).astype(BF16)

    y = jnp.dot(act_ref[...], wd_ref[...], preferred_element_type=F32)
    x2 = x_ref[0] + mod_ref[0, 5:6, :] * y
    ms = jnp.mean(x2 * x2, axis=-1, keepdims=True)
    o_ref[0] = x2 * lax.rsqrt(ms + EPS) * gf_ref[...]


def _conv_ffn(x1, mod3, g2, w_up, ffn_w, ffn_b, w_down, g_final, tm):
    B, S, D = x1.shape
    d_ff = w_down.shape[0]
    hb = tm // FFN_HALO
    n_hb = S // FFN_HALO
    return pl.pallas_call(
        functools.partial(_ffn_kernel, tm=tm, d_ff=d_ff),
        grid=(B, S // tm),
        in_specs=[
            pl.BlockSpec((1, tm, D), lambda b, i: (b, i, 0)),
            pl.BlockSpec((1, FFN_HALO, D), lambda b, i: (b, jnp.maximum(i * hb - 1, 0), 0)),
            pl.BlockSpec((1, FFN_HALO, D), lambda b, i: (b, jnp.minimum((i + 1) * hb, n_hb - 1), 0)),
            pl.BlockSpec((1, 6, D), lambda b, i: (b, 0, 0)),
            _resident((1, D)),
            _resident(w_up.shape),
            _resident(ffn_w.shape),
            _resident((1, 2 * d_ff)),
            _resident(w_down.shape),
            _resident((1, D)),
        ],
        out_specs=pl.BlockSpec((1, tm, D), lambda b, i: (b, i, 0)),
        out_shape=jax.ShapeDtypeStruct((B, S, D), F32),
        scratch_shapes=[
            pltpu.VMEM((tm + 2 * FFN_HALO, D), BF16),
            pltpu.VMEM((tm, d_ff), BF16),
        ],
        compiler_params=pltpu.CompilerParams(
            dimension_semantics=("arbitrary", "arbitrary"), vmem_limit_bytes=VMEM_LIMIT_BYTES),
        name="conv_ffn_final_norm",
    )(x1, x1, x1, mod3, g2, w_up, ffn_w, ffn_b, w_down, g_final)


def kernel(x, c, ctx, c_ctx, w_mod, b_mod, g_norm1, w_in, rpb, conv_w, conv_b, ln_g, ln_b, w_out,
           g_norm2, w_up, ffn_conv_w, ffn_conv_b, w_down, g_final):
    B, S, D = x.shape
    depth = w_mod.shape[0]
    assert depth == 1, "the context-stream update of deeper stacks is not implemented"
    assert S % GRID_W == 0 and D_ATTN + conv_w.shape[-1] == D
    l = 0
    row = lambda a: a.reshape(1, -1)

    mod_rows = 8
    cc = jnp.concatenate([c, c_ctx[None], jnp.zeros((mod_rows - B - 1, D), c.dtype)], axis=0)
    mod3 = _modulation(cc, w_mod[l], row(b_mod[l])).reshape(mod_rows, 6, D)

    g1 = row(g_norm1[l])
    q, k, v, u = _input_projection(x, mod3, g1, w_in[l], min(PROJ_TILE, S))
    k_c, v_c = _context_kv(ctx, mod3, g1, w_in[l], ctx_row=B)

    y_na = _attention(q, k, v, k_c, v_c, _bias_classes(rpb[l]), rows_per_tile=ATTN_ROWS_PER_TILE)

    x1, w_up_b, w_down_b = _mix_out(
        y_na, u, x, mod3, conv_w[l], row(conv_b[l]), row(ln_g[l]), row(ln_b[l]),
        w_out[l].astype(BF16), w_up[l], w_down[l], min(MIX_TILE, S))

    return _conv_ffn(x1, mod3, row(g_norm2[l]), w_up_b, ffn_conv_w[l],
                     row(ffn_conv_b[l]), w_down_b, row(g_final), min(PROJ_TILE, S))
```

```python
import functools

import numpy as np
import jax
import jax.numpy as jnp
from jax import lax
from jax.experimental import pallas as pl
from jax.experimental.pallas import tpu as pltpu

F32 = jnp.float32
BF16 = jnp.bfloat16

GRID_W = 64
N_HEADS = 8
HEAD_DIM = 64
D_ATTN = N_HEADS * HEAD_DIM
CONV_WIDTH = 31
NA_ROWS = 8
NA_COLS = 16
FFN_CONV_WIDTH = 3
EPS = 1e-6
ATTN_SCALE = HEAD_DIM ** -0.5
LOG2_E = 1.4426950408889634

HEADS_PER_GROUP = 4
GROUP_W = HEADS_PER_GROUP * HEAD_DIM
N_BIAS_CLASSES = NA_ROWS
CONV_HALO = 16
FFN_HALO = 8
CONV_SPLIT = 16
MIX_ROW_BLOCK = 128
MIX_SLOTS = 2
FF_CHUNK = 256
FFN_DOWN_ROWS = 256
MOD_TILE = 2048
PROJ_TILE = 1024
MIX_TILE = 1024
ATTN_ROWS_PER_TILE = 16
ATTN_ROW_UNROLL = 16

VMEM_LIMIT_BYTES = 56 * 1024 * 1024


def _sigmoid(x):
    return 1.0 / (1.0 + jnp.exp(-x))


def _norm_modulate(x, gain, shift, scale):
    ms = jnp.mean(x * x, axis=-1, keepdims=True)
    return (x * lax.rsqrt(ms + EPS)) * (gain * (1.0 + scale)) + shift


def _resident(shape):
    nd = len(shape)
    return pl.BlockSpec(shape, lambda *_: (0,) * nd, pipeline_mode=pl.Buffered(1))


def _mod_kernel(c_ref, w_ref, b_ref, o_ref):
    c = c_ref[...]
    a = (c * _sigmoid(c)).astype(BF16)
    o_ref[...] = jnp.dot(a, w_ref[...].astype(BF16), preferred_element_type=F32) + b_ref[...]


def _modulation(cc, w_mod, b_mod):
    rows, d = cc.shape
    n = w_mod.shape[1]
    tn = MOD_TILE
    return pl.pallas_call(
        _mod_kernel,
        grid=(n // tn,),
        in_specs=[
            pl.BlockSpec((rows, d), lambda j: (0, 0)),
            pl.BlockSpec((d, tn), lambda j: (0, j)),
            pl.BlockSpec((1, tn), lambda j: (0, j)),
        ],
        out_specs=pl.BlockSpec((rows, tn), lambda j: (0, j)),
        out_shape=jax.ShapeDtypeStruct((rows, n), F32),
        compiler_params=pltpu.CompilerParams(
            dimension_semantics=("arbitrary",), vmem_limit_bytes=VMEM_LIMIT_BYTES),
        name="modulation",
    )(cc, w_mod, b_mod)


def _inproj_kernel(x_ref, mod_ref, g_ref, w_ref, q_ref, k_ref, v_ref, u_ref):
    h = _norm_modulate(x_ref[0], g_ref[...], mod_ref[0, 0:1, :], mod_ref[0, 1:2, :]).astype(BF16)

    def proj(lo):
        return jnp.dot(h, w_ref[:, lo:lo + D_ATTN].astype(BF16), preferred_element_type=F32)

    g = proj(4 * D_ATTN)
    a = proj(3 * D_ATTN)
    u_ref[0] = (a * _sigmoid(g)).astype(BF16)
    q_ref[0] = (proj(0) * (ATTN_SCALE * LOG2_E)).astype(BF16)
    k_ref[0] = proj(D_ATTN).astype(BF16)
    v_ref[0] = proj(2 * D_ATTN).astype(BF16)


def _ctx_kv_kernel(x_ref, mod_ref, g_ref, wk_ref, wv_ref, k_ref, v_ref):
    h = _norm_modulate(x_ref[0], g_ref[...], mod_ref[0, 0:1, :], mod_ref[0, 1:2, :]).astype(BF16)
    k_ref[0] = jnp.dot(h, wk_ref[...].astype(BF16), preferred_element_type=F32).astype(BF16)
    v_ref[0] = jnp.dot(h, wv_ref[...].astype(BF16), preferred_element_type=F32).astype(BF16)


def _input_projection(x, mod3, g1, w_in, tm):
    B, S, D = x.shape
    out = jax.ShapeDtypeStruct((B, S, D_ATTN), BF16)
    tile = pl.BlockSpec((1, tm, D_ATTN), lambda b, i: (b, i, 0))
    return pl.pallas_call(
        _inproj_kernel,
        grid=(B, S // tm),
        in_specs=[
            pl.BlockSpec((1, tm, D), lambda b, i: (b, i, 0)),
            pl.BlockSpec((1, 6, D), lambda b, i: (b, 0, 0)),
            _resident((1, D)),
            _resident(w_in.shape),
        ],
        out_specs=[tile, tile, tile, tile],
        out_shape=[out, out, out, out],
        compiler_params=pltpu.CompilerParams(
            dimension_semantics=("arbitrary", "arbitrary"), vmem_limit_bytes=VMEM_LIMIT_BYTES),
        name="input_projection",
    )(x, mod3, g1, w_in)


def _context_kv(ctx, mod3, g1, w_in, ctx_row):
    B, L, D = ctx.shape
    out = jax.ShapeDtypeStruct((B, L, D_ATTN), BF16)
    tile = pl.BlockSpec((1, L, D_ATTN), lambda b: (b, 0, 0))
    return pl.pallas_call(
        _ctx_kv_kernel,
        grid=(B,),
        in_specs=[
            pl.BlockSpec((1, L, D), lambda b: (b, 0, 0)),
            pl.BlockSpec((1, 6, D), lambda b: (ctx_row, 0, 0)),
            _resident((1, D)),
            pl.BlockSpec((D, D_ATTN), lambda b: (0, 1), pipeline_mode=pl.Buffered(1)),
            pl.BlockSpec((D, D_ATTN), lambda b: (0, 2), pipeline_mode=pl.Buffered(1)),
        ],
        out_specs=[tile, tile],
        out_shape=[out, out],
        compiler_params=pltpu.CompilerParams(
            dimension_semantics=("arbitrary",), vmem_limit_bytes=VMEM_LIMIT_BYTES),
        name="context_kv",
    )(ctx, mod3, g1, w_in, w_in)


def _bias_classes(rpb):
    H, n_ro, n_co = rpb.shape
    half = n_co // 2
    period = 2 * GRID_W
    ro_pad = n_ro + 1
    z = jnp.concatenate(
        [rpb[..., half:], jnp.zeros((H, n_ro, period - n_co), rpb.dtype), rpb[..., :half]], axis=-1)
    z = jnp.pad(z.astype(F32), ((0, 0), (0, ro_pad - n_ro), (0, 0))).reshape(H * ro_pad, period)
    q = np.arange(GRID_W)
    start = np.clip(q - NA_COLS // 2, 0, GRID_W - NA_COLS)
    kc = np.arange(period) % GRID_W
    valid = (kc[None, :] >= start[:, None]) & (kc[None, :] < start[:, None] + NA_COLS)
    window = jnp.asarray(np.where(valid, 0.0, -np.inf), F32)
    return pl.pallas_call(
        functools.partial(_bias_kernel, n_ro=n_ro),
        grid=(H,),
        in_specs=[
            pl.BlockSpec((ro_pad, period), lambda h: (h, 0)),
            pl.BlockSpec((GRID_W, period), lambda h: (0, 0)),
        ],
        out_specs=pl.BlockSpec((N_BIAS_CLASSES, 1, GRID_W, NA_ROWS * GRID_W), lambda h: (0, h, 0, 0)),
        out_shape=jax.ShapeDtypeStruct((N_BIAS_CLASSES, H, GRID_W, NA_ROWS * GRID_W), F32),
        compiler_params=pltpu.CompilerParams(dimension_semantics=("arbitrary",)),
        name="bias_classes",
    )(z, window)


def _bias_kernel(z_ref, window_ref, o_ref, *, n_ro):
    period = z_ref.shape[-1]
    low_half = lax.broadcasted_iota(jnp.int32, (GRID_W, period), 1) < GRID_W
    toep = [pltpu.roll(jnp.broadcast_to(z_ref[ro:ro + 1, :], (GRID_W, period)), 0, 1, stride=1, stride_axis=0)
            for ro in range(n_ro)]
    pairs = [jnp.where(low_half, toep[ro], pltpu.roll(toep[ro + 1], GRID_W, 1)) * LOG2_E + window_ref[...]
             for ro in range(n_ro - 1)]
    for c in range(N_BIAS_CLASSES):
        for jp in range(NA_ROWS // 2):
            o_ref[c, 0, :, jp * period:(jp + 1) * period] = pairs[c + 2 * jp]


def _attn_kernel(q_ref, k_ref, v_ref, kc_ref, vc_ref, bias_ref, o_ref, *, rows_per_tile, n_rows):
    i = pl.program_id(1)
    n_local = NA_ROWS * GRID_W
    lane_head = lax.broadcasted_iota(jnp.int32, (GRID_W, GROUP_W), 1) // HEAD_DIM
    nt = (((1,), (1,)), ((), ()))
    def row_body(r, carry):
        r_abs = i * rows_per_tile + r
        kr0 = jnp.clip(r_abs - NA_ROWS // 2, 0, n_rows - NA_ROWS)
        cls = kr0 - r_abs + (NA_ROWS - 1)
        key0 = pl.multiple_of(kr0 * GRID_W, GRID_W)
        q0 = pl.multiple_of(r * GRID_W, GRID_W)
        for g in range(N_HEADS // HEADS_PER_GROUP):
            ls = slice(g * GROUP_W, (g + 1) * GROUP_W)
            kc = kc_ref[0, :, ls]
            vc = vc_ref[0, :, ls]
            qg = q_ref[0, pl.ds(q0, GRID_W), ls]
            kl = k_ref[0, pl.ds(key0, n_local), ls]
            vl = v_ref[0, pl.ds(key0, n_local), ls]
            qs = jnp.concatenate(
                [jnp.where(lane_head == hh, qg, 0) for hh in range(HEADS_PER_GROUP)], axis=0)
            s_loc = lax.dot_general(qs, kl, nt, preferred_element_type=F32)
            s_ctx = lax.dot_general(qs, kc, nt, preferred_element_type=F32)
            bias = bias_ref[cls, g * HEADS_PER_GROUP:(g + 1) * HEADS_PER_GROUP]
            s_loc = s_loc + bias.reshape(HEADS_PER_GROUP * GRID_W, n_local)
            s = jnp.concatenate([s_loc, s_ctx], axis=1)
            p = jnp.exp2(s - jnp.max(s, axis=-1, keepdims=True))
            denom = jnp.sum(p, axis=-1, keepdims=True)
            pb = p.astype(BF16)
            o = (jnp.dot(pb[:, :n_local], vl, preferred_element_type=F32)
                 + jnp.dot(pb[:, n_local:], vc, preferred_element_type=F32))
            o = o * (1.0 / denom)
            y = o[0:GRID_W]
            for hh in range(1, HEADS_PER_GROUP):
                y = jnp.where(lane_head == hh, o[hh * GRID_W:(hh + 1) * GRID_W], y)
            o_ref[0, pl.ds(q0, GRID_W), ls] = y.astype(BF16)
        return carry

    lax.fori_loop(0, rows_per_tile, row_body, 0, unroll=ATTN_ROW_UNROLL)


def _attention(q, k, v, kc, vc, bias, rows_per_tile):
    B, S, _ = q.shape
    L = kc.shape[1]
    n_rows = S // GRID_W
    tm = rows_per_tile * GRID_W
    return pl.pallas_call(
        functools.partial(_attn_kernel, rows_per_tile=rows_per_tile, n_rows=n_rows),
        grid=(B, n_rows // rows_per_tile),
        in_specs=[
            pl.BlockSpec((1, tm, D_ATTN), lambda b, i: (b, i, 0)),
            pl.BlockSpec((1, S, D_ATTN), lambda b, i: (b, 0, 0)),
            pl.BlockSpec((1, S, D_ATTN), lambda b, i: (b, 0, 0)),
            pl.BlockSpec((1, L, D_ATTN), lambda b, i: (b, 0, 0)),
            pl.BlockSpec((1, L, D_ATTN), lambda b, i: (b, 0, 0)),
            _resident(bias.shape),
        ],
        out_specs=pl.BlockSpec((1, tm, D_ATTN), lambda b, i: (b, i, 0)),
        out_shape=jax.ShapeDtypeStruct((B, S, D_ATTN), BF16),
        compiler_params=pltpu.CompilerParams(
            dimension_semantics=("arbitrary", "arbitrary"), vmem_limit_bytes=VMEM_LIMIT_BYTES),
        name="neighbourhood_attention",
    )(q, k, v, kc, vc, bias)


def _shift_sum_matrix():
    span = MIX_ROW_BLOCK + CONV_SPLIT
    s_mat = np.zeros((MIX_ROW_BLOCK, CONV_SPLIT * span), np.float32)
    t = np.arange(MIX_ROW_BLOCK)
    for s in range(CONV_SPLIT):
        s_mat[t, s * span + t + s] = 1.0
    return jnp.asarray(s_mat, BF16)


def _mix_out_kernel(yna_ref, u_ref, up_ref, un_ref, x_ref, mod_ref, cw_ref, cb_ref, lg_ref, lb_ref,
                    shift_ref, w_ref, wu_ref, wd_ref, o_ref, wub_ref, wdb_ref, ext_ref, part_ref, ycv_ref,
                    *, tm):
    wub_ref[...] = wu_ref[...].astype(BF16)
    wdb_ref[...] = wd_ref[...].astype(BF16)
    i = pl.program_id(1)
    last = pl.num_programs(1) - 1
    zero = jnp.zeros((CONV_HALO, u_ref.shape[-1]), BF16)
    ext_ref[0:CONV_HALO] = jnp.where(i > 0, up_ref[0], zero)
    ext_ref[CONV_HALO:CONV_HALO + tm] = u_ref[0]
    ext_ref[CONV_HALO + tm:] = jnp.where(i < last, un_ref[0], zero)

    cb = cb_ref[...]
    lg = lg_ref[...]
    lb = lb_ref[...]
    d_attn = yna_ref.shape[-1]
    gate = mod_ref[0, 2:3, :]
    span = MIX_ROW_BLOCK + CONV_SPLIT
    n_tiles = span // CONV_SPLIT
    first_off = CONV_HALO - CONV_WIDTH // 2

    def partials(blk, slot):
        r0 = pl.multiple_of(blk * MIX_ROW_BLOCK, MIX_ROW_BLOCK)
        win = [ext_ref[pl.ds(r0 + v * CONV_SPLIT, CONV_SPLIT), :] for v in range(n_tiles + 1)]
        for s in range(CONV_SPLIT):
            taps = [(m, CONV_SPLIT * m + s - first_off) for m in range(2)]
            taps = [(m, j) for m, j in taps if 0 <= j < CONV_WIDTH]
            weights = [cw_ref[j] for _, j in taps]
            for v in range(n_tiles):
                part = weights[0] * win[v + taps[0][0]]
                for (m, _), w in zip(taps[1:], weights[1:]):
                    part = part + w * win[v + m]
                part_ref[slot, s * span + v * CONV_SPLIT:s * span + (v + 1) * CONV_SPLIT, :] = part

    def conv_norm(slot):
        acc = jnp.dot(shift_ref[...], part_ref[slot], preferred_element_type=F32) + cb
        mu = jnp.mean(acc, axis=-1, keepdims=True)
        xc = acc - mu
        var = jnp.mean(xc * xc, axis=-1, keepdims=True)
        y = xc * lax.rsqrt(var + EPS) * lg + lb
        ycv_ref[slot] = (y * _sigmoid(y)).astype(BF16)

    def project(blk, slot):
        rows = pl.ds(pl.multiple_of(blk * MIX_ROW_BLOCK, MIX_ROW_BLOCK), MIX_ROW_BLOCK)
        proj = (jnp.dot(yna_ref[0, rows, :], w_ref[0:d_attn, :], preferred_element_type=F32)
                + jnp.dot(ycv_ref[slot], w_ref[d_attn:, :], preferred_element_type=F32))
        o_ref[0, rows, :] = x_ref[0, rows, :] + gate * proj

    n_blocks = tm // MIX_ROW_BLOCK
    assert n_blocks >= 3
    partials(0, 0)
    partials(1, 1)
    conv_norm(0)
    conv_norm(1)
    partials(2, 0)
    for b in range(n_blocks):
        project(b, b % 2)
        if b + 2 < n_blocks:
            conv_norm(b % 2)
        if b + 3 < n_blocks:
            partials(b + 3, (b + 1) % 2)


def _mix_out(y_na, u, x, mod3, conv_w, conv_b, ln_g, ln_b, w_out, w_up, w_down, tm):
    B, S, D = x.shape
    d_conv = u.shape[-1]
    hb = tm // CONV_HALO
    n_hb = S // CONV_HALO
    cw_rep = jnp.broadcast_to(conv_w.astype(BF16)[:, None, :], (CONV_WIDTH, CONV_SPLIT, d_conv))
    shift = _shift_sum_matrix()
    n_steps = S // tm
    d_ff = w_down.shape[0]
    n_slabs = d_ff // FF_CHUNK
    assert B * n_steps >= n_slabs and w_up.shape[1] == 2 * d_ff
    slab = lambda b, i: jnp.minimum(b * n_steps + i, n_slabs - 1)
    return pl.pallas_call(
        functools.partial(_mix_out_kernel, tm=tm),
        grid=(B, S // tm),
        in_specs=[
            pl.BlockSpec((1, tm, D_ATTN), lambda b, i: (b, i, 0)),
            pl.BlockSpec((1, tm, d_conv), lambda b, i: (b, i, 0)),
            pl.BlockSpec((1, CONV_HALO, d_conv), lambda b, i: (b, jnp.maximum(i * hb - 1, 0), 0)),
            pl.BlockSpec((1, CONV_HALO, d_conv), lambda b, i: (b, jnp.minimum((i + 1) * hb, n_hb - 1), 0)),
            pl.BlockSpec((1, tm, D), lambda b, i: (b, i, 0)),
            pl.BlockSpec((1, 6, D), lambda b, i: (b, 0, 0)),
            _resident(cw_rep.shape),
            _resident((1, d_conv)),
            _resident((1, d_conv)),
            _resident((1, d_conv)),
            _resident(shift.shape),
            _resident(w_out.shape),
            pl.BlockSpec((D, 2 * FF_CHUNK), lambda b, i: (0, slab(b, i))),
            pl.BlockSpec((FF_CHUNK, D), lambda b, i: (slab(b, i), 0)),
        ],
        out_specs=[
            pl.BlockSpec((1, tm, D), lambda b, i: (b, i, 0)),
            pl.BlockSpec((D, 2 * FF_CHUNK), lambda b, i: (0, slab(b, i))),
            pl.BlockSpec((FF_CHUNK, D), lambda b, i: (slab(b, i), 0)),
        ],
        out_shape=[
            jax.ShapeDtypeStruct((B, S, D), F32),
            jax.ShapeDtypeStruct(w_up.shape, BF16),
            jax.ShapeDtypeStruct(w_down.shape, BF16),
        ],
        scratch_shapes=[
            pltpu.VMEM((tm + 2 * CONV_HALO, d_conv), BF16),
            pltpu.VMEM((MIX_SLOTS, shift.shape[1], d_conv), BF16),
            pltpu.VMEM((MIX_SLOTS, MIX_ROW_BLOCK, d_conv), BF16),
        ],
        compiler_params=pltpu.CompilerParams(
            dimension_semantics=("arbitrary", "arbitrary"), vmem_limit_bytes=VMEM_LIMIT_BYTES),
        name="conv_mixer_out_projection",
    )(y_na, u, u, u, x, mod3, cw_rep, conv_b, ln_g, ln_b, shift, w_out, w_up, w_down)


def _ffn_kernel(x_ref, xp_ref, xn_ref, mod_ref, g2_ref, wu_ref, fw_ref, fb_ref, wd_ref, gf_ref,
                o_ref, h_ref, act_ref, *, tm, d_ff):
    i = pl.program_id(1)
    last = pl.num_programs(1) - 1
    gain = g2_ref[...]
    shift = mod_ref[0, 3:4, :]
    scale = mod_ref[0, 4:5, :]
    hp = _norm_modulate(xp_ref[0], gain, shift, scale)
    hn = _norm_modulate(xn_ref[0], gain, shift, scale)
    h_ref[0:FFN_HALO] = jnp.where(i > 0, hp, 0.0).astype(BF16)
    h_ref[FFN_HALO:FFN_HALO + tm] = _norm_modulate(x_ref[0], gain, shift, scale).astype(BF16)
    h_ref[FFN_HALO + tm:] = jnp.where(i < last, hn, 0.0).astype(BF16)

    n_ext = tm + 2 * FFN_HALO

    def conv(lo):
        up = jnp.dot(h_ref[...], wu_ref[:, lo:lo + FF_CHUNK], preferred_element_type=F32)
        before = pltpu.roll(up, 1, axis=0)
        after = pltpu.roll(up, n_ext - 1, axis=0)
        mid = slice(FFN_HALO, FFN_HALO + tm)
        w = fw_ref[:, lo:lo + FF_CHUNK]
        return (w[0:1] * before[mid] + w[1:2] * up[mid] + w[2:3] * after[mid]
                + fb_ref[:, lo:lo + FF_CHUNK])

    for f in range(d_ff // FF_CHUNK):
        gate = conv(f * FF_CHUNK)
        val = conv(d_ff + f * FF_CHUNK)
        act_ref[:, f * FF_CHUNK:(f + 1) * FF_CHUNK] = (gate * _sigmoid(gate) * val).astype(BF16)

    for r in range(tm // FFN_DOWN_ROWS):
        rows = slice(r * FFN_DOWN_ROWS, (r + 1) * FFN_DOWN_ROWS)
        y = jnp.dot(act_ref[rows, :], wd_ref[...], preferred_element_type=F32)
        x2 = x_ref[0, rows, :] + mod_ref[0, 5:6, :] * y
        ms = jnp.mean(x2 * x2, axis=-1, keepdims=True)
        o_ref[0, rows, :] = x2 * lax.rsqrt(ms + EPS) * gf_ref[...]


def _conv_ffn(x1, mod3, g2, w_up, ffn_w, ffn_b, w_down, g_final, tm):
    B, S, D = x1.shape
    d_ff = w_down.shape[0]
    hb = tm // FFN_HALO
    n_hb = S // FFN_HALO
    return pl.pallas_call(
        functools.partial(_ffn_kernel, tm=tm, d_ff=d_ff),
        grid=(B, S // tm),
        in_specs=[
            pl.BlockSpec((1, tm, D), lambda b, i: (b, i, 0)),
            pl.BlockSpec((1, FFN_HALO, D), lambda b, i: (b, jnp.maximum(i * hb - 1, 0), 0)),
            pl.BlockSpec((1, FFN_HALO, D), lambda b, i: (b, jnp.minimum((i + 1) * hb, n_hb - 1), 0)),
            pl.BlockSpec((1, 6, D), lambda b, i: (b, 0, 0)),
            _resident((1, D)),
            _resident(w_up.shape),
            _resident(ffn_w.shape),
            _resident((1, 2 * d_ff)),
            _resident(w_down.shape),
            _resident((1, D)),
        ],
        out_specs=pl.BlockSpec((1, tm, D), lambda b, i: (b, i, 0)),
        out_shape=jax.ShapeDtypeStruct((B, S, D), F32),
        scratch_shapes=[
            pltpu.VMEM((tm + 2 * FFN_HALO, D), BF16),
            pltpu.VMEM((tm, d_ff), BF16),
        ],
        compiler_params=pltpu.CompilerParams(
            dimension_semantics=("arbitrary", "arbitrary"), vmem_limit_bytes=VMEM_LIMIT_BYTES),
        name="conv_ffn_final_norm",
    )(x1, x1, x1, mod3, g2, w_up, ffn_w, ffn_b, w_down, g_final)


def kernel(x, c, ctx, c_ctx, w_mod, b_mod, g_norm1, w_in, rpb, conv_w, conv_b, ln_g, ln_b, w_out,
           g_norm2, w_up, ffn_conv_w, ffn_conv_b, w_down, g_final):
    B, S, D = x.shape
    depth = w_mod.shape[0]
    assert depth == 1, "the context-stream update of deeper stacks is not implemented"
    assert S % GRID_W == 0 and D_ATTN + conv_w.shape[-1] == D
    l = 0
    row = lambda a: a.reshape(1, -1)

    mod_rows = 8
    cc = jnp.concatenate([c, c_ctx[None], jnp.zeros((mod_rows - B - 1, D), c.dtype)], axis=0)
    mod3 = _modulation(cc, w_mod[l], row(b_mod[l])).reshape(mod_rows, 6, D)

    g1 = row(g_norm1[l])
    q, k, v, u = _input_projection(x, mod3, g1, w_in[l], min(PROJ_TILE, S))
    k_c, v_c = _context_kv(ctx, mod3, g1, w_in[l], ctx_row=B)

    y_na = _attention(q, k, v, k_c, v_c, _bias_classes(rpb[l]), rows_per_tile=ATTN_ROWS_PER_TILE)

    x1, w_up_b, w_down_b = _mix_out(
        y_na, u, x, mod3, conv_w[l], row(conv_b[l]), row(ln_g[l]), row(ln_b[l]),
        w_out[l].astype(BF16), w_up[l], w_down[l], min(MIX_TILE, S))

    return _conv_ffn(x1, mod3, row(g_norm2[l]), w_up_b, ffn_conv_w[l],
                     row(ffn_conv_b[l]), w_down_b, row(g_final), min(PROJ_TILE, S))
```

```python
import functools

import numpy as np
import jax
import jax.numpy as jnp
from jax import lax
from jax.experimental import pallas as pl
from jax.experimental.pallas import tpu as pltpu

F32 = jnp.float32
BF16 = jnp.bfloat16

GRID_W = 64
N_HEADS = 8
HEAD_DIM = 64
D_ATTN = N_HEADS * HEAD_DIM
CONV_WIDTH = 31
NA_ROWS = 8
NA_COLS = 16
FFN_CONV_WIDTH = 3
EPS = 1e-6
ATTN_SCALE = HEAD_DIM ** -0.5
LOG2_E = 1.4426950408889634

HEADS_PER_GROUP = 4
GROUP_W = HEADS_PER_GROUP * HEAD_DIM
N_BIAS_CLASSES = NA_ROWS
CONV_HALO = 16
FFN_HALO = 8
CONV_SPLIT = 16
MIX_ROW_BLOCK = 128
MIX_SLOTS = 2
FF_CHUNK = 256
FFN_DOWN_ROWS = 256
MOD_TILE = 2048
PROJ_TILE = 1024
MIX_TILE = 1024
ATTN_ROWS_PER_TILE = 16
ATTN_ROW_UNROLL = 16

VMEM_LIMIT_BYTES = 56 * 1024 * 1024


def _sigmoid(x):
    return 1.0 / (1.0 + jnp.exp(-x))


def _norm_modulate(x, gain, shift, scale):
    ms = jnp.mean(x * x, axis=-1, keepdims=True)
    return (x * lax.rsqrt(ms + EPS)) * (gain * (1.0 + scale)) + shift


def _resident(shape):
    nd = len(shape)
    return pl.BlockSpec(shape, lambda *_: (0,) * nd, pipeline_mode=pl.Buffered(1))


def _mod_kernel(c_ref, w_ref, b_ref, o_ref):
    c = c_ref[...]
    a = (c * _sigmoid(c)).astype(BF16)
    o_ref[...] = jnp.dot(a, w_ref[...].astype(BF16), preferred_element_type=F32) + b_ref[...]


def _modulation(cc, w_mod, b_mod, n):
    rows, d = cc.shape
    tn = min(MOD_TILE, n)
    return pl.pallas_call(
        _mod_kernel,
        grid=(n // tn,),
        in_specs=[
            pl.BlockSpec((rows, d), lambda j: (0, 0)),
            pl.BlockSpec((d, tn), lambda j: (0, j)),
            pl.BlockSpec((1, tn), lambda j: (0, j)),
        ],
        out_specs=pl.BlockSpec((rows, tn), lambda j: (0, j)),
        out_shape=jax.ShapeDtypeStruct((rows, n), F32),
        compiler_params=pltpu.CompilerParams(
            dimension_semantics=("arbitrary",), vmem_limit_bytes=VMEM_LIMIT_BYTES),
        name="modulation",
    )(cc, w_mod, b_mod)


def _inproj_kernel(x_ref, mod_ref, g_ref, w_ref, cc_ref, wm_ref, bm_ref, q_ref, k_ref, v_ref, u_ref,
                   modr_ref):
    _mod_kernel(cc_ref, wm_ref, bm_ref, modr_ref)
    h = _norm_modulate(x_ref[0], g_ref[...], mod_ref[0, 0:1, :], mod_ref[0, 1:2, :]).astype(BF16)

    def proj(lo):
        return jnp.dot(h, w_ref[:, lo:lo + D_ATTN].astype(BF16), preferred_element_type=F32)

    g = proj(4 * D_ATTN)
    a = proj(3 * D_ATTN)
    u_ref[0] = (a * _sigmoid(g)).astype(BF16)
    q_ref[0] = (proj(0) * (ATTN_SCALE * LOG2_E)).astype(BF16)
    k_ref[0] = proj(D_ATTN).astype(BF16)
    v_ref[0] = proj(2 * D_ATTN).astype(BF16)


def _ctx_kv_kernel(x_ref, mod_ref, g_ref, wk_ref, wv_ref, k_ref, v_ref):
    h = _norm_modulate(x_ref[0], g_ref[...], mod_ref[0, 0:1, :], mod_ref[0, 1:2, :]).astype(BF16)
    k_ref[0] = jnp.dot(h, wk_ref[...].astype(BF16), preferred_element_type=F32).astype(BF16)
    v_ref[0] = jnp.dot(h, wv_ref[...].astype(BF16), preferred_element_type=F32).astype(BF16)


def _input_projection(x, mod_first, g1, w_in, cc, w_mod, b_mod, tm):
    B, S, D = x.shape
    n_steps = S // tm
    n_first = mod_first.shape[1] * D
    n_rest = w_mod.shape[1] - n_first
    slab = n_rest // (B * n_steps)
    assert slab * B * n_steps == n_rest and slab % 128 == 0 and n_first % slab == 0
    out = jax.ShapeDtypeStruct((B, S, D_ATTN), BF16)
    tile = pl.BlockSpec((1, tm, D_ATTN), lambda b, i: (b, i, 0))
    return pl.pallas_call(
        _inproj_kernel,
        grid=(B, n_steps),
        in_specs=[
            pl.BlockSpec((1, tm, D), lambda b, i: (b, i, 0)),
            pl.BlockSpec((1, mod_first.shape[1], D), lambda b, i: (b, 0, 0)),
            _resident((1, D)),
            _resident(w_in.shape),
            _resident(cc.shape),
            pl.BlockSpec((D, slab), lambda b, i: (0, n_first // slab + b * n_steps + i)),
            pl.BlockSpec((1, slab), lambda b, i: (0, n_first // slab + b * n_steps + i)),
        ],
        out_specs=[tile, tile, tile, tile, pl.BlockSpec((cc.shape[0], slab), lambda b, i: (0, b * n_steps + i))],
        out_shape=[out, out, out, out, jax.ShapeDtypeStruct((cc.shape[0], n_rest), F32)],
        compiler_params=pltpu.CompilerParams(
            dimension_semantics=("arbitrary", "arbitrary"), vmem_limit_bytes=VMEM_LIMIT_BYTES),
        name="input_projection",
    )(x, mod_first, g1, w_in, cc, w_mod, b_mod)


def _context_kv(ctx, mod3, g1, w_in, ctx_row):
    B, L, D = ctx.shape
    out = jax.ShapeDtypeStruct((B, L, D_ATTN), BF16)
    tile = pl.BlockSpec((1, L, D_ATTN), lambda b: (b, 0, 0))
    return pl.pallas_call(
        _ctx_kv_kernel,
        grid=(B,),
        in_specs=[
            pl.BlockSpec((1, L, D), lambda b: (b, 0, 0)),
            pl.BlockSpec((1, mod3.shape[1], D), lambda b: (ctx_row, 0, 0)),
            _resident((1, D)),
            pl.BlockSpec((D, D_ATTN), lambda b: (0, 1), pipeline_mode=pl.Buffered(1)),
            pl.BlockSpec((D, D_ATTN), lambda b: (0, 2), pipeline_mode=pl.Buffered(1)),
        ],
        out_specs=[tile, tile],
        out_shape=[out, out],
        compiler_params=pltpu.CompilerParams(
            dimension_semantics=("arbitrary",), vmem_limit_bytes=VMEM_LIMIT_BYTES),
        name="context_kv",
    )(ctx, mod3, g1, w_in, w_in)


def _bias_operands(rpb):
    H, n_ro, n_co = rpb.shape
    half = n_co // 2
    period = 2 * GRID_W
    ro_pad = n_ro + 1
    z = jnp.concatenate(
        [rpb[..., half:], jnp.zeros((H, n_ro, period - n_co), rpb.dtype), rpb[..., :half]], axis=-1)
    z = jnp.pad(z.astype(F32), ((0, 0), (0, ro_pad - n_ro), (0, 0))).reshape(H * ro_pad, period)
    q = np.arange(GRID_W)
    start = np.clip(q - NA_COLS // 2, 0, GRID_W - NA_COLS)
    kc = np.arange(period) % GRID_W
    valid = (kc[None, :] >= start[:, None]) & (kc[None, :] < start[:, None] + NA_COLS)
    window = jnp.asarray(np.where(valid, 0.0, -np.inf), F32)
    return z, window


def _fill_bias_classes(z_ref, window_ref, bias_ref):
    period = z_ref.shape[-1]
    n_heads = bias_ref.shape[1]
    ro_pad = z_ref.shape[0] // n_heads
    n_ro = ro_pad - 1
    low_half = lax.broadcasted_iota(jnp.int32, (GRID_W, period), 1) < GRID_W
    for h in range(n_heads):
        toep = [pltpu.roll(jnp.broadcast_to(z_ref[h * ro_pad + ro:h * ro_pad + ro + 1, :], (GRID_W, period)),
                           0, 1, stride=1, stride_axis=0) for ro in range(n_ro)]
        pairs = [jnp.where(low_half, toep[ro], pltpu.roll(toep[ro + 1], GRID_W, 1)) * LOG2_E + window_ref[...]
                 for ro in range(n_ro - 1)]
        for c in range(N_BIAS_CLASSES):
            for jp in range(NA_ROWS // 2):
                bias_ref[c, h, :, jp * period:(jp + 1) * period] = pairs[c + 2 * jp]


def _attn_kernel(q_ref, k_ref, v_ref, kc_ref, vc_ref, z_ref, window_ref, o_ref, bias_ref, *,
                 rows_per_tile, n_rows):
    i = pl.program_id(1)

    @pl.when((pl.program_id(0) == 0) & (i == 0))
    def _():
        _fill_bias_classes(z_ref, window_ref, bias_ref)

    n_local = NA_ROWS * GRID_W
    lane_head = lax.broadcasted_iota(jnp.int32, (GRID_W, GROUP_W), 1) // HEAD_DIM
    nt = (((1,), (1,)), ((), ()))
    def row_body(r, carry):
        r_abs = i * rows_per_tile + r
        kr0 = jnp.clip(r_abs - NA_ROWS // 2, 0, n_rows - NA_ROWS)
        cls = kr0 - r_abs + (NA_ROWS - 1)
        key0 = pl.multiple_of(kr0 * GRID_W, GRID_W)
        q0 = pl.multiple_of(r * GRID_W, GRID_W)
        for g in range(N_HEADS // HEADS_PER_GROUP):
            ls = slice(g * GROUP_W, (g + 1) * GROUP_W)
            kc = kc_ref[0, :, ls]
            vc = vc_ref[0, :, ls]
            qg = q_ref[0, pl.ds(q0, GRID_W), ls]
            kl = k_ref[0, pl.ds(key0, n_local), ls]
            vl = v_ref[0, pl.ds(key0, n_local), ls]
            qs = jnp.concatenate(
                [jnp.where(lane_head == hh, qg, 0) for hh in range(HEADS_PER_GROUP)], axis=0)
            s_loc = lax.dot_general(qs, kl, nt, preferred_element_type=F32)
            s_ctx = lax.dot_general(qs, kc, nt, preferred_element_type=F32)
            bias = bias_ref[cls, g * HEADS_PER_GROUP:(g + 1) * HEADS_PER_GROUP]
            s_loc = s_loc + bias.reshape(HEADS_PER_GROUP * GRID_W, n_local)
            s = jnp.concatenate([s_loc, s_ctx], axis=1)
            p = jnp.exp2(s - jnp.max(s, axis=-1, keepdims=True))
            denom = jnp.sum(p, axis=-1, keepdims=True)
            pb = p.astype(BF16)
            o = (jnp.dot(pb[:, :n_local], vl, preferred_element_type=F32)
                 + jnp.dot(pb[:, n_local:], vc, preferred_element_type=F32))
            o = o * (1.0 / denom)
            y = o[0:GRID_W]
            for hh in range(1, HEADS_PER_GROUP):
                y = jnp.where(lane_head == hh, o[hh * GRID_W:(hh + 1) * GRID_W], y)
            o_ref[0, pl.ds(q0, GRID_W), ls] = y.astype(BF16)
        return carry

    lax.fori_loop(0, rows_per_tile, row_body, 0, unroll=ATTN_ROW_UNROLL)


def _attention(q, k, v, kc, vc, rpb, rows_per_tile):
    B, S, _ = q.shape
    L = kc.shape[1]
    n_rows = S // GRID_W
    tm = rows_per_tile * GRID_W
    z, window = _bias_operands(rpb)
    return pl.pallas_call(
        functools.partial(_attn_kernel, rows_per_tile=rows_per_tile, n_rows=n_rows),
        grid=(B, n_rows // rows_per_tile),
        in_specs=[
            pl.BlockSpec((1, tm, D_ATTN), lambda b, i: (b, i, 0)),
            pl.BlockSpec((1, S, D_ATTN), lambda b, i: (b, 0, 0)),
            pl.BlockSpec((1, S, D_ATTN), lambda b, i: (b, 0, 0)),
            pl.BlockSpec((1, L, D_ATTN), lambda b, i: (b, 0, 0)),
            pl.BlockSpec((1, L, D_ATTN), lambda b, i: (b, 0, 0)),
            _resident(z.shape),
            _resident(window.shape),
        ],
        out_specs=pl.BlockSpec((1, tm, D_ATTN), lambda b, i: (b, i, 0)),
        out_shape=jax.ShapeDtypeStruct((B, S, D_ATTN), BF16),
        scratch_shapes=[pltpu.VMEM((N_BIAS_CLASSES, rpb.shape[0], GRID_W, NA_ROWS * GRID_W), F32)],
        compiler_params=pltpu.CompilerParams(
            dimension_semantics=("arbitrary", "arbitrary"), vmem_limit_bytes=VMEM_LIMIT_BYTES),
        name="neighbourhood_attention",
    )(q, k, v, kc, vc, z, window)


def _shift_sum_matrix():
    span = MIX_ROW_BLOCK + CONV_SPLIT
    s_mat = np.zeros((MIX_ROW_BLOCK, CONV_SPLIT * span), np.float32)
    t = np.arange(MIX_ROW_BLOCK)
    for s in range(CONV_SPLIT):
        s_mat[t, s * span + t + s] = 1.0
    return jnp.asarray(s_mat, BF16)


def _mix_out_kernel(yna_ref, u_ref, up_ref, un_ref, x_ref, mod_ref, cw_ref, cb_ref, lg_ref, lb_ref,
                    shift_ref, w_ref, wu_ref, wd_ref, o_ref, wub_ref, wdb_ref, ext_ref, part_ref, ycv_ref,
                    *, tm):
    wub_ref[...] = wu_ref[...].astype(BF16)
    wdb_ref[...] = wd_ref[...].astype(BF16)
    i = pl.program_id(1)
    last = pl.num_programs(1) - 1
    zero = jnp.zeros((CONV_HALO, u_ref.shape[-1]), BF16)
    ext_ref[0:CONV_HALO] = jnp.where(i > 0, up_ref[0], zero)
    ext_ref[CONV_HALO:CONV_HALO + tm] = u_ref[0]
    ext_ref[CONV_HALO + tm:] = jnp.where(i < last, un_ref[0], zero)

    cb = cb_ref[...]
    lg = lg_ref[...]
    lb = lb_ref[...]
    d_attn = yna_ref.shape[-1]
    gate = mod_ref[0, 2:3, :]
    span = MIX_ROW_BLOCK + CONV_SPLIT
    n_tiles = span // CONV_SPLIT
    first_off = CONV_HALO - CONV_WIDTH // 2

    def partials(blk, slot):
        r0 = pl.multiple_of(blk * MIX_ROW_BLOCK, MIX_ROW_BLOCK)
        win = [ext_ref[pl.ds(r0 + v * CONV_SPLIT, CONV_SPLIT), :] for v in range(n_tiles + 1)]
        for s in range(CONV_SPLIT):
            taps = [(m, CONV_SPLIT * m + s - first_off) for m in range(2)]
            taps = [(m, j) for m, j in taps if 0 <= j < CONV_WIDTH]
            weights = [cw_ref[j] for _, j in taps]
            for v in range(n_tiles):
                part = weights[0] * win[v + taps[0][0]]
                for (m, _), w in zip(taps[1:], weights[1:]):
                    part = part + w * win[v + m]
                part_ref[slot, s * span + v * CONV_SPLIT:s * span + (v + 1) * CONV_SPLIT, :] = part

    def conv_norm(slot):
        acc = jnp.dot(shift_ref[...], part_ref[slot], preferred_element_type=F32) + cb
        mu = jnp.mean(acc, axis=-1, keepdims=True)
        xc = acc - mu
        var = jnp.mean(xc * xc, axis=-1, keepdims=True)
        y = xc * lax.rsqrt(var + EPS) * lg + lb
        ycv_ref[slot] = (y * _sigmoid(y)).astype(BF16)

    def project(blk, slot):
        rows = pl.ds(pl.multiple_of(blk * MIX_ROW_BLOCK, MIX_ROW_BLOCK), MIX_ROW_BLOCK)
        proj = (jnp.dot(yna_ref[0, rows, :], w_ref[0:d_attn, :], preferred_element_type=F32)
                + jnp.dot(ycv_ref[slot], w_ref[d_attn:, :], preferred_element_type=F32))
        o_ref[0, rows, :] = x_ref[0, rows, :] + gate * proj

    n_blocks = tm // MIX_ROW_BLOCK
    assert n_blocks >= 3
    partials(0, 0)
    partials(1, 1)
    conv_norm(0)
    conv_norm(1)
    partials(2, 0)
    for b in range(n_blocks):
        project(b, b % 2)
        if b + 2 < n_blocks:
            conv_norm(b % 2)
        if b + 3 < n_blocks:
            partials(b + 3, (b + 1) % 2)


def _mix_out(y_na, u, x, mod3, conv_w, conv_b, ln_g, ln_b, w_out, w_up, w_down, tm):
    B, S, D = x.shape
    d_conv = u.shape[-1]
    hb = tm // CONV_HALO
    n_hb = S // CONV_HALO
    cw_rep = jnp.broadcast_to(conv_w.astype(BF16)[:, None, :], (CONV_WIDTH, CONV_SPLIT, d_conv))
    shift = _shift_sum_matrix()
    n_steps = S // tm
    d_ff = w_down.shape[0]
    n_slabs = d_ff // FF_CHUNK
    assert B * n_steps >= n_slabs and w_up.shape[1] == 2 * d_ff
    slab = lambda b, i: jnp.minimum(b * n_steps + i, n_slabs - 1)
    return pl.pallas_call(
        functools.partial(_mix_out_kernel, tm=tm),
        grid=(B, S // tm),
        in_specs=[
            pl.BlockSpec((1, tm, D_ATTN), lambda b, i: (b, i, 0)),
            pl.BlockSpec((1, tm, d_conv), lambda b, i: (b, i, 0)),
            pl.BlockSpec((1, CONV_HALO, d_conv), lambda b, i: (b, jnp.maximum(i * hb - 1, 0), 0)),
            pl.BlockSpec((1, CONV_HALO, d_conv), lambda b, i: (b, jnp.minimum((i + 1) * hb, n_hb - 1), 0)),
            pl.BlockSpec((1, tm, D), lambda b, i: (b, i, 0)),
            pl.BlockSpec((1, 6, D), lambda b, i: (b, 0, 0)),
            _resident(cw_rep.shape),
            _resident((1, d_conv)),
            _resident((1, d_conv)),
            _resident((1, d_conv)),
            _resident(shift.shape),
            _resident(w_out.shape),
            pl.BlockSpec((D, 2 * FF_CHUNK), lambda b, i: (0, slab(b, i))),
            pl.BlockSpec((FF_CHUNK, D), lambda b, i: (slab(b, i), 0)),
        ],
        out_specs=[
            pl.BlockSpec((1, tm, D), lambda b, i: (b, i, 0)),
            pl.BlockSpec((D, 2 * FF_CHUNK), lambda b, i: (0, slab(b, i))),
            pl.BlockSpec((FF_CHUNK, D), lambda b, i: (slab(b, i), 0)),
        ],
        out_shape=[
            jax.ShapeDtypeStruct((B, S, D), F32),
            jax.ShapeDtypeStruct(w_up.shape, BF16),
            jax.ShapeDtypeStruct(w_down.shape, BF16),
        ],
        scratch_shapes=[
            pltpu.VMEM((tm + 2 * CONV_HALO, d_conv), BF16),
            pltpu.VMEM((MIX_SLOTS, shift.shape[1], d_conv), BF16),
            pltpu.VMEM((MIX_SLOTS, MIX_ROW_BLOCK, d_conv), BF16),
        ],
        compiler_params=pltpu.CompilerParams(
            dimension_semantics=("arbitrary", "arbitrary"), vmem_limit_bytes=VMEM_LIMIT_BYTES),
        name="conv_mixer_out_projection",
    )(y_na, u, u, u, x, mod3, cw_rep, conv_b, ln_g, ln_b, shift, w_out, w_up, w_down)


def _ffn_kernel(x_ref, xp_ref, xn_ref, mod_ref, g2_ref, wu_ref, fw_ref, fb_ref, wd_ref, gf_ref,
                o_ref, h_ref, act_ref, *, tm, d_ff):
    i = pl.program_id(1)
    last = pl.num_programs(1) - 1
    gain = g2_ref[...]
    shift = mod_ref[0, 3:4, :]
    scale = mod_ref[0, 4:5, :]
    hp = _norm_modulate(xp_ref[0], gain, shift, scale)
    hn = _norm_modulate(xn_ref[0], gain, shift, scale)
    h_ref[0:FFN_HALO] = jnp.where(i > 0, hp, 0.0).astype(BF16)
    h_ref[FFN_HALO:FFN_HALO + tm] = _norm_modulate(x_ref[0], gain, shift, scale).astype(BF16)
    h_ref[FFN_HALO + tm:] = jnp.where(i < last, hn, 0.0).astype(BF16)

    n_ext = tm + 2 * FFN_HALO

    def conv(lo):
        up = jnp.dot(h_ref[...], wu_ref[:, lo:lo + FF_CHUNK], preferred_element_type=F32)
        before = pltpu.roll(up, 1, axis=0)
        after = pltpu.roll(up, n_ext - 1, axis=0)
        mid = slice(FFN_HALO, FFN_HALO + tm)
        w = fw_ref[:, lo:lo + FF_CHUNK]
        return (w[0:1] * before[mid] + w[1:2] * up[mid] + w[2:3] * after[mid]
                + fb_ref[:, lo:lo + FF_CHUNK])

    for f in range(d_ff // FF_CHUNK):
        gate = conv(f * FF_CHUNK)
        val = conv(d_ff + f * FF_CHUNK)
        act_ref[:, f * FF_CHUNK:(f + 1) * FF_CHUNK] = (gate * _sigmoid(gate) * val).astype(BF16)

    for r in range(tm // FFN_DOWN_ROWS):
        rows = slice(r * FFN_DOWN_ROWS, (r + 1) * FFN_DOWN_ROWS)
        y = jnp.dot(act_ref[rows, :], wd_ref[...], preferred_element_type=F32)
        x2 = x_ref[0, rows, :] + mod_ref[0, 5:6, :] * y
        ms = jnp.mean(x2 * x2, axis=-1, keepdims=True)
        o_ref[0, rows, :] = x2 * lax.rsqrt(ms + EPS) * gf_ref[...]


def _conv_ffn(x1, mod3, g2, w_up, ffn_w, ffn_b, w_down, g_final, tm):
    B, S, D = x1.shape
    d_ff = w_down.shape[0]
    hb = tm // FFN_HALO
    n_hb = S // FFN_HALO
    return pl.pallas_call(
        functools.partial(_ffn_kernel, tm=tm, d_ff=d_ff),
        grid=(B, S // tm),
        in_specs=[
            pl.BlockSpec((1, tm, D), lambda b, i: (b, i, 0)),
            pl.BlockSpec((1, FFN_HALO, D), lambda b, i: (b, jnp.maximum(i * hb - 1, 0), 0)),
            pl.BlockSpec((1, FFN_HALO, D), lambda b, i: (b, jnp.minimum((i + 1) * hb, n_hb - 1), 0)),
            pl.BlockSpec((1, 6, D), lambda b, i: (b, 0, 0)),
            _resident((1, D)),
            _resident(w_up.shape),
            _resident(ffn_w.shape),
            _resident((1, 2 * d_ff)),
            _resident(w_down.shape),
            _resident((1, D)),
        ],
        out_specs=pl.BlockSpec((1, tm, D), lambda b, i: (b, i, 0)),
        out_shape=jax.ShapeDtypeStruct((B, S, D), F32),
        scratch_shapes=[
            pltpu.VMEM((tm + 2 * FFN_HALO, D), BF16),
            pltpu.VMEM((tm, d_ff), BF16),
        ],
        compiler_params=pltpu.CompilerParams(
            dimension_semantics=("arbitrary", "arbitrary"), vmem_limit_bytes=VMEM_LIMIT_BYTES),
        name="conv_ffn_final_norm",
    )(x1, x1, x1, mod3, g2, w_up, ffn_w, ffn_b, w_down, g_final)


def kernel(x, c, ctx, c_ctx, w_mod, b_mod, g_norm1, w_in, rpb, conv_w, conv_b, ln_g, ln_b, w_out,
           g_norm2, w_up, ffn_conv_w, ffn_conv_b, w_down, g_final):
    B, S, D = x.shape
    depth = w_mod.shape[0]
    assert depth == 1, "the context-stream update of deeper stacks is not implemented"
    assert S % GRID_W == 0 and D_ATTN + conv_w.shape[-1] == D
    l = 0
    row = lambda a: a.reshape(1, -1)

    mod_rows = 8
    cc = jnp.concatenate([c, c_ctx[None], jnp.zeros((mod_rows - B - 1, D), c.dtype)], axis=0)
    mod_first = _modulation(cc, w_mod[l], row(b_mod[l]), 2 * D).reshape(mod_rows, 2, D)

    g1 = row(g_norm1[l])
    q, k, v, u, mod_rest = _input_projection(x, mod_first, g1, w_in[l], cc, w_mod[l], row(b_mod[l]),
                                             min(PROJ_TILE, S))
    mod3 = jnp.concatenate([mod_first, mod_rest.reshape(mod_rows, 4, D)], axis=1)
    k_c, v_c = _context_kv(ctx, mod_first, g1, w_in[l], ctx_row=B)

    y_na = _attention(q, k, v, k_c, v_c, rpb[l], rows_per_tile=ATTN_ROWS_PER_TILE)

    x1, w_up_b, w_down_b = _mix_out(
        y_na, u, x, mod3, conv_w[l], row(conv_b[l]), row(ln_g[l]), row(ln_b[l]),
        w_out[l].astype(BF16), w_up[l], w_down[l], min(MIX_TILE, S))

    return _conv_ffn(x1, mod3, row(g_norm2[l]), w_up_b, ffn_conv_w[l],
                     row(ffn_conv_b[l]), w_down_b, row(g_final), min(PROJ_TILE, S))
```

```python
import functools

import numpy as np
import jax
import jax.numpy as jnp
from jax import lax
from jax.experimental import pallas as pl
from jax.experimental.pallas import tpu as pltpu

F32 = jnp.float32
BF16 = jnp.bfloat16

GRID_W = 64
N_HEADS = 8
HEAD_DIM = 64
D_ATTN = N_HEADS * HEAD_DIM
CONV_WIDTH = 31
NA_ROWS = 8
NA_COLS = 16
FFN_CONV_WIDTH = 3
EPS = 1e-6
ATTN_SCALE = HEAD_DIM ** -0.5
LOG2_E = 1.4426950408889634

HEADS_PER_GROUP = 4
GROUP_W = HEADS_PER_GROUP * HEAD_DIM
N_BIAS_CLASSES = NA_ROWS
CONV_HALO = 16
FFN_HALO = 8
CONV_SPLIT = 16
MIX_ROW_BLOCK = 128
MIX_SLOTS = 2
FF_CHUNK = 256
FFN_DOWN_ROWS = 256
MOD_TILE = 2048
PROJ_TILE = 1024
MIX_TILE = 1024
ATTN_ROWS_PER_TILE = 32
ATTN_ROW_UNROLL = 16

VMEM_LIMIT_BYTES = 56 * 1024 * 1024


def _sigmoid(x):
    return 1.0 / (1.0 + jnp.exp(-x))


def _norm_modulate(x, gain, shift, scale):
    ms = jnp.mean(x * x, axis=-1, keepdims=True)
    return (x * lax.rsqrt(ms + EPS)) * (gain * (1.0 + scale)) + shift


def _resident(shape):
    nd = len(shape)
    return pl.BlockSpec(shape, lambda *_: (0,) * nd, pipeline_mode=pl.Buffered(1))


def _mod_kernel(c_ref, w_ref, b_ref, o_ref):
    c = c_ref[...]
    a = (c * _sigmoid(c)).astype(BF16)
    o_ref[...] = jnp.dot(a, w_ref[...].astype(BF16), preferred_element_type=F32) + b_ref[...]


def _modulation(cc, w_mod, b_mod, n):
    rows, d = cc.shape
    tn = min(MOD_TILE, n)
    return pl.pallas_call(
        _mod_kernel,
        grid=(n // tn,),
        in_specs=[
            pl.BlockSpec((rows, d), lambda j: (0, 0)),
            pl.BlockSpec((d, tn), lambda j: (0, j)),
            pl.BlockSpec((1, tn), lambda j: (0, j)),
        ],
        out_specs=pl.BlockSpec((rows, tn), lambda j: (0, j)),
        out_shape=jax.ShapeDtypeStruct((rows, n), F32),
        compiler_params=pltpu.CompilerParams(
            dimension_semantics=("arbitrary",), vmem_limit_bytes=VMEM_LIMIT_BYTES),
        name="modulation",
    )(cc, w_mod, b_mod)


def _inproj_kernel(x_ref, mod_ref, g_ref, w_ref, cc_ref, wm_ref, bm_ref, q_ref, k_ref, v_ref, u_ref,
                   modr_ref):
    _mod_kernel(cc_ref, wm_ref, bm_ref, modr_ref)
    h = _norm_modulate(x_ref[0], g_ref[...], mod_ref[0, 0:1, :], mod_ref[0, 1:2, :]).astype(BF16)

    def proj(lo):
        return jnp.dot(h, w_ref[:, lo:lo + D_ATTN].astype(BF16), preferred_element_type=F32)

    g = proj(4 * D_ATTN)
    a = proj(3 * D_ATTN)
    u_ref[0] = (a * _sigmoid(g)).astype(BF16)
    q_ref[0] = (proj(0) * (ATTN_SCALE * LOG2_E)).astype(BF16)
    k_ref[0] = proj(D_ATTN).astype(BF16)
    v_ref[0] = proj(2 * D_ATTN).astype(BF16)


def _ctx_kv_kernel(x_ref, mod_ref, g_ref, wk_ref, wv_ref, k_ref, v_ref):
    h = _norm_modulate(x_ref[0], g_ref[...], mod_ref[0, 0:1, :], mod_ref[0, 1:2, :]).astype(BF16)
    k_ref[0] = jnp.dot(h, wk_ref[...].astype(BF16), preferred_element_type=F32).astype(BF16)
    v_ref[0] = jnp.dot(h, wv_ref[...].astype(BF16), preferred_element_type=F32).astype(BF16)


def _input_projection(x, mod_first, g1, w_in, cc, w_mod, b_mod, tm):
    B, S, D = x.shape
    n_steps = S // tm
    n_first = mod_first.shape[1] * D
    n_rest = w_mod.shape[1] - n_first
    slab = n_rest // (B * n_steps)
    assert slab * B * n_steps == n_rest and slab % 128 == 0 and n_first % slab == 0
    out = jax.ShapeDtypeStruct((B, S, D_ATTN), BF16)
    tile = pl.BlockSpec((1, tm, D_ATTN), lambda b, i: (b, i, 0))
    return pl.pallas_call(
        _inproj_kernel,
        grid=(B, n_steps),
        in_specs=[
            pl.BlockSpec((1, tm, D), lambda b, i: (b, i, 0)),
            pl.BlockSpec((1, mod_first.shape[1], D), lambda b, i: (b, 0, 0)),
            _resident((1, D)),
            _resident(w_in.shape),
            _resident(cc.shape),
            pl.BlockSpec((D, slab), lambda b, i: (0, n_first // slab + b * n_steps + i)),
            pl.BlockSpec((1, slab), lambda b, i: (0, n_first // slab + b * n_steps + i)),
        ],
        out_specs=[tile, tile, tile, tile, pl.BlockSpec((cc.shape[0], slab), lambda b, i: (0, b * n_steps + i))],
        out_shape=[out, out, out, out, jax.ShapeDtypeStruct((cc.shape[0], n_rest), F32)],
        compiler_params=pltpu.CompilerParams(
            dimension_semantics=("arbitrary", "arbitrary"), vmem_limit_bytes=VMEM_LIMIT_BYTES),
        name="input_projection",
    )(x, mod_first, g1, w_in, cc, w_mod, b_mod)


def _context_kv(ctx, mod3, g1, w_in, ctx_row):
    B, L, D = ctx.shape
    out = jax.ShapeDtypeStruct((B, L, D_ATTN), BF16)
    tile = pl.BlockSpec((1, L, D_ATTN), lambda b: (b, 0, 0))
    return pl.pallas_call(
        _ctx_kv_kernel,
        grid=(B,),
        in_specs=[
            pl.BlockSpec((1, L, D), lambda b: (b, 0, 0)),
            pl.BlockSpec((1, mod3.shape[1], D), lambda b: (ctx_row, 0, 0)),
            _resident((1, D)),
            pl.BlockSpec((D, D_ATTN), lambda b: (0, 1), pipeline_mode=pl.Buffered(1)),
            pl.BlockSpec((D, D_ATTN), lambda b: (0, 2), pipeline_mode=pl.Buffered(1)),
        ],
        out_specs=[tile, tile],
        out_shape=[out, out],
        compiler_params=pltpu.CompilerParams(
            dimension_semantics=("arbitrary",), vmem_limit_bytes=VMEM_LIMIT_BYTES),
        name="context_kv",
    )(ctx, mod3, g1, w_in, w_in)


def _bias_operands(rpb):
    H, n_ro, n_co = rpb.shape
    half = n_co // 2
    period = 2 * GRID_W
    ro_pad = n_ro + 1
    z = jnp.concatenate(
        [rpb[..., half:], jnp.zeros((H, n_ro, period - n_co), rpb.dtype), rpb[..., :half]], axis=-1)
    z = jnp.pad(z.astype(F32), ((0, 0), (0, ro_pad - n_ro), (0, 0))).reshape(H * ro_pad, period)
    q = np.arange(GRID_W)
    start = np.clip(q - NA_COLS // 2, 0, GRID_W - NA_COLS)
    kc = np.arange(period) % GRID_W
    valid = (kc[None, :] >= start[:, None]) & (kc[None, :] < start[:, None] + NA_COLS)
    window = jnp.asarray(np.where(valid, 0.0, -np.inf), F32)
    return z, window


def _fill_bias_classes(z_ref, window_ref, bias_ref):
    period = z_ref.shape[-1]
    n_heads = bias_ref.shape[1]
    ro_pad = z_ref.shape[0] // n_heads
    n_ro = ro_pad - 1
    low_half = lax.broadcasted_iota(jnp.int32, (GRID_W, period), 1) < GRID_W
    for h in range(n_heads):
        toep = [pltpu.roll(jnp.broadcast_to(z_ref[h * ro_pad + ro:h * ro_pad + ro + 1, :], (GRID_W, period)),
                           0, 1, stride=1, stride_axis=0) for ro in range(n_ro)]
        pairs = [jnp.where(low_half, toep[ro], pltpu.roll(toep[ro + 1], GRID_W, 1)) * LOG2_E + window_ref[...]
                 for ro in range(n_ro - 1)]
        for c in range(N_BIAS_CLASSES):
            for jp in range(NA_ROWS // 2):
                bias_ref[c, h, :, jp * period:(jp + 1) * period] = pairs[c + 2 * jp]


def _attn_kernel(q_ref, k_ref, v_ref, kc_ref, vc_ref, z_ref, window_ref, o_ref, bias_ref, *,
                 rows_per_tile, n_rows):
    i = pl.program_id(1)

    @pl.when((pl.program_id(0) == 0) & (i == 0))
    def _():
        _fill_bias_classes(z_ref, window_ref, bias_ref)

    n_local = NA_ROWS * GRID_W
    lane_head = lax.broadcasted_iota(jnp.int32, (GRID_W, GROUP_W), 1) // HEAD_DIM
    nt = (((1,), (1,)), ((), ()))
    def row_body(r, carry):
        r_abs = i * rows_per_tile + r
        kr0 = jnp.clip(r_abs - NA_ROWS // 2, 0, n_rows - NA_ROWS)
        cls = kr0 - r_abs + (NA_ROWS - 1)
        key0 = pl.multiple_of(kr0 * GRID_W, GRID_W)
        q0 = pl.multiple_of(r * GRID_W, GRID_W)
        for g in range(N_HEADS // HEADS_PER_GROUP):
            ls = slice(g * GROUP_W, (g + 1) * GROUP_W)
            kc = kc_ref[0, :, ls]
            vc = vc_ref[0, :, ls]
            qg = q_ref[0, pl.ds(q0, GRID_W), ls]
            kl = k_ref[0, pl.ds(key0, n_local), ls]
            vl = v_ref[0, pl.ds(key0, n_local), ls]
            qs = jnp.concatenate(
                [jnp.where(lane_head == hh, qg, 0) for hh in range(HEADS_PER_GROUP)], axis=0)
            s_loc = lax.dot_general(qs, kl, nt, preferred_element_type=F32)
            s_ctx = lax.dot_general(qs, kc, nt, preferred_element_type=F32)
            bias = bias_ref[cls, g * HEADS_PER_GROUP:(g + 1) * HEADS_PER_GROUP]
            s_loc = s_loc + bias.reshape(HEADS_PER_GROUP * GRID_W, n_local)
            s = jnp.concatenate([s_loc, s_ctx], axis=1)
            p = jnp.exp2(s - jnp.max(s, axis=-1, keepdims=True))
            denom = jnp.sum(p, axis=-1, keepdims=True)
            pb = p.astype(BF16)
            o = (jnp.dot(pb[:, :n_local], vl, preferred_element_type=F32)
                 + jnp.dot(pb[:, n_local:], vc, preferred_element_type=F32))
            o = o * (1.0 / denom)
            y = o[0:GRID_W]
            for hh in range(1, HEADS_PER_GROUP):
                y = jnp.where(lane_head == hh, o[hh * GRID_W:(hh + 1) * GRID_W], y)
            o_ref[0, pl.ds(q0, GRID_W), ls] = y.astype(BF16)
        return carry

    lax.fori_loop(0, rows_per_tile, row_body, 0, unroll=ATTN_ROW_UNROLL)


def _attention(q, k, v, kc, vc, rpb, rows_per_tile):
    B, S, _ = q.shape
    L = kc.shape[1]
    n_rows = S // GRID_W
    tm = rows_per_tile * GRID_W
    z, window = _bias_operands(rpb)
    return pl.pallas_call(
        functools.partial(_attn_kernel, rows_per_tile=rows_per_tile, n_rows=n_rows),
        grid=(B, n_rows // rows_per_tile),
        in_specs=[
            pl.BlockSpec((1, tm, D_ATTN), lambda b, i: (b, i, 0)),
            pl.BlockSpec((1, S, D_ATTN), lambda b, i: (b, 0, 0)),
            pl.BlockSpec((1, S, D_ATTN), lambda b, i: (b, 0, 0)),
            pl.BlockSpec((1, L, D_ATTN), lambda b, i: (b, 0, 0)),
            pl.BlockSpec((1, L, D_ATTN), lambda b, i: (b, 0, 0)),
            _resident(z.shape),
            _resident(window.shape),
        ],
        out_specs=pl.BlockSpec((1, tm, D_ATTN), lambda b, i: (b, i, 0)),
        out_shape=jax.ShapeDtypeStruct((B, S, D_ATTN), BF16),
        scratch_shapes=[pltpu.VMEM((N_BIAS_CLASSES, rpb.shape[0], GRID_W, NA_ROWS * GRID_W), F32)],
        compiler_params=pltpu.CompilerParams(
            dimension_semantics=("arbitrary", "arbitrary"), vmem_limit_bytes=VMEM_LIMIT_BYTES),
        name="neighbourhood_attention",
    )(q, k, v, kc, vc, z, window)


def _shift_sum_matrix():
    span = MIX_ROW_BLOCK + CONV_SPLIT
    s_mat = np.zeros((MIX_ROW_BLOCK, CONV_SPLIT * span), np.float32)
    t = np.arange(MIX_ROW_BLOCK)
    for s in range(CONV_SPLIT):
        s_mat[t, s * span + t + s] = 1.0
    return jnp.asarray(s_mat, BF16)


def _mix_out_kernel(yna_ref, u_ref, up_ref, un_ref, x_ref, mod_ref, cw_ref, cb_ref, lg_ref, lb_ref,
                    shift_ref, w_ref, wu_ref, wd_ref, o_ref, wub_ref, wdb_ref, ext_ref, part_ref, ycv_ref,
                    *, tm):
    wub_ref[...] = wu_ref[...].astype(BF16)
    wdb_ref[...] = wd_ref[...].astype(BF16)
    i = pl.program_id(1)
    last = pl.num_programs(1) - 1
    zero = jnp.zeros((CONV_HALO, u_ref.shape[-1]), BF16)
    ext_ref[0:CONV_HALO] = jnp.where(i > 0, up_ref[0], zero)
    ext_ref[CONV_HALO:CONV_HALO + tm] = u_ref[0]
    ext_ref[CONV_HALO + tm:] = jnp.where(i < last, un_ref[0], zero)

    cb = cb_ref[...]
    lg = lg_ref[...]
    lb = lb_ref[...]
    d_attn = yna_ref.shape[-1]
    gate = mod_ref[0, 2:3, :]
    span = MIX_ROW_BLOCK + CONV_SPLIT
    n_tiles = span // CONV_SPLIT
    first_off = CONV_HALO - CONV_WIDTH // 2

    def partials(blk, slot):
        r0 = pl.multiple_of(blk * MIX_ROW_BLOCK, MIX_ROW_BLOCK)
        win = [ext_ref[pl.ds(r0 + v * CONV_SPLIT, CONV_SPLIT), :] for v in range(n_tiles + 1)]
        for s in range(CONV_SPLIT):
            taps = [(m, CONV_SPLIT * m + s - first_off) for m in range(2)]
            taps = [(m, j) for m, j in taps if 0 <= j < CONV_WIDTH]
            weights = [cw_ref[j] for _, j in taps]
            for v in range(n_tiles):
                part = weights[0] * win[v + taps[0][0]]
                for (m, _), w in zip(taps[1:], weights[1:]):
                    part = part + w * win[v + m]
                part_ref[slot, s * span + v * CONV_SPLIT:s * span + (v + 1) * CONV_SPLIT, :] = part

    def conv_norm(slot):
        acc = jnp.dot(shift_ref[...], part_ref[slot], preferred_element_type=F32) + cb
        mu = jnp.mean(acc, axis=-1, keepdims=True)
        xc = acc - mu
        var = jnp.mean(xc * xc, axis=-1, keepdims=True)
        y = xc * lax.rsqrt(var + EPS) * lg + lb
        ycv_ref[slot] = (y * _sigmoid(y)).astype(BF16)

    def project(blk, slot):
        rows = pl.ds(pl.multiple_of(blk * MIX_ROW_BLOCK, MIX_ROW_BLOCK), MIX_ROW_BLOCK)
        proj = (jnp.dot(yna_ref[0, rows, :], w_ref[0:d_attn, :], preferred_element_type=F32)
                + jnp.dot(ycv_ref[slot], w_ref[d_attn:, :], preferred_element_type=F32))
        o_ref[0, rows, :] = x_ref[0, rows, :] + gate * proj

    n_blocks = tm // MIX_ROW_BLOCK
    assert n_blocks >= 3
    partials(0, 0)
    partials(1, 1)
    conv_norm(0)
    conv_norm(1)
    partials(2, 0)
    for b in range(n_blocks):
        project(b, b % 2)
        if b + 2 < n_blocks:
            conv_norm(b % 2)
        if b + 3 < n_blocks:
            partials(b + 3, (b + 1) % 2)


def _mix_out(y_na, u, x, mod3, conv_w, conv_b, ln_g, ln_b, w_out, w_up, w_down, tm):
    B, S, D = x.shape
    d_conv = u.shape[-1]
    hb = tm // CONV_HALO
    n_hb = S // CONV_HALO
    cw_rep = jnp.broadcast_to(conv_w.astype(BF16)[:, None, :], (CONV_WIDTH, CONV_SPLIT, d_conv))
    shift = _shift_sum_matrix()
    n_steps = S // tm
    d_ff = w_down.shape[0]
    n_slabs = d_ff // FF_CHUNK
    assert B * n_steps >= n_slabs and w_up.shape[1] == 2 * d_ff
    slab = lambda b, i: jnp.minimum(b * n_steps + i, n_slabs - 1)
    return pl.pallas_call(
        functools.partial(_mix_out_kernel, tm=tm),
        grid=(B, S // tm),
        in_specs=[
            pl.BlockSpec((1, tm, D_ATTN), lambda b, i: (b, i, 0)),
            pl.BlockSpec((1, tm, d_conv), lambda b, i: (b, i, 0)),
            pl.BlockSpec((1, CONV_HALO, d_conv), lambda b, i: (b, jnp.maximum(i * hb - 1, 0), 0)),
            pl.BlockSpec((1, CONV_HALO, d_conv), lambda b, i: (b, jnp.minimum((i + 1) * hb, n_hb - 1), 0)),
            pl.BlockSpec((1, tm, D), lambda b, i: (b, i, 0)),
            pl.BlockSpec((1, 6, D), lambda b, i: (b, 0, 0)),
            _resident(cw_rep.shape),
            _resident((1, d_conv)),
            _resident((1, d_conv)),
            _resident((1, d_conv)),
            _resident(shift.shape),
            _resident(w_out.shape),
            pl.BlockSpec((D, 2 * FF_CHUNK), lambda b, i: (0, slab(b, i))),
            pl.BlockSpec((FF_CHUNK, D), lambda b, i: (slab(b, i), 0)),
        ],
        out_specs=[
            pl.BlockSpec((1, tm, D), lambda b, i: (b, i, 0)),
            pl.BlockSpec((D, 2 * FF_CHUNK), lambda b, i: (0, slab(b, i))),
            pl.BlockSpec((FF_CHUNK, D), lambda b, i: (slab(b, i), 0)),
        ],
        out_shape=[
            jax.ShapeDtypeStruct((B, S, D), F32),
            jax.ShapeDtypeStruct(w_up.shape, BF16),
            jax.ShapeDtypeStruct(w_down.shape, BF16),
        ],
        scratch_shapes=[
            pltpu.VMEM((tm + 2 * CONV_HALO, d_conv), BF16),
            pltpu.VMEM((MIX_SLOTS, shift.shape[1], d_conv), BF16),
            pltpu.VMEM((MIX_SLOTS, MIX_ROW_BLOCK, d_conv), BF16),
        ],
        compiler_params=pltpu.CompilerParams(
            dimension_semantics=("arbitrary", "arbitrary"), vmem_limit_bytes=VMEM_LIMIT_BYTES),
        name="conv_mixer_out_projection",
    )(y_na, u, u, u, x, mod3, cw_rep, conv_b, ln_g, ln_b, shift, w_out, w_up, w_down)


def _ffn_kernel(x_ref, xp_ref, xn_ref, mod_ref, g2_ref, wu_ref, fw_ref, fb_ref, wd_ref, gf_ref,
                o_ref, h_ref, act_ref, *, tm, d_ff):
    i = pl.program_id(1)
    last = pl.num_programs(1) - 1
    gain = g2_ref[...]
    shift = mod_ref[0, 3:4, :]
    scale = mod_ref[0, 4:5, :]
    hp = _norm_modulate(xp_ref[0], gain, shift, scale)
    hn = _norm_modulate(xn_ref[0], gain, shift, scale)
    h_ref[0:FFN_HALO] = jnp.where(i > 0, hp, 0.0).astype(BF16)
    h_ref[FFN_HALO:FFN_HALO + tm] = _norm_modulate(x_ref[0], gain, shift, scale).astype(BF16)
    h_ref[FFN_HALO + tm:] = jnp.where(i < last, hn, 0.0).astype(BF16)

    n_ext = tm + 2 * FFN_HALO

    def conv(lo):
        up = jnp.dot(h_ref[...], wu_ref[:, lo:lo + FF_CHUNK], preferred_element_type=F32)
        before = pltpu.roll(up, 1, axis=0)
        after = pltpu.roll(up, n_ext - 1, axis=0)
        mid = slice(FFN_HALO, FFN_HALO + tm)
        w = fw_ref[:, lo:lo + FF_CHUNK]
        return (w[0:1] * before[mid] + w[1:2] * up[mid] + w[2:3] * after[mid]
                + fb_ref[:, lo:lo + FF_CHUNK])

    for f in range(d_ff // FF_CHUNK):
        gate = conv(f * FF_CHUNK)
        val = conv(d_ff + f * FF_CHUNK)
        act_ref[:, f * FF_CHUNK:(f + 1) * FF_CHUNK] = (gate * _sigmoid(gate) * val).astype(BF16)

    for r in range(tm // FFN_DOWN_ROWS):
        rows = slice(r * FFN_DOWN_ROWS, (r + 1) * FFN_DOWN_ROWS)
        y = jnp.dot(act_ref[rows, :], wd_ref[...], preferred_element_type=F32)
        x2 = x_ref[0, rows, :] + mod_ref[0, 5:6, :] * y
        ms = jnp.mean(x2 * x2, axis=-1, keepdims=True)
        o_ref[0, rows, :] = x2 * lax.rsqrt(ms + EPS) * gf_ref[...]


def _conv_ffn(x1, mod3, g2, w_up, ffn_w, ffn_b, w_down, g_final, tm):
    B, S, D = x1.shape
    d_ff = w_down.shape[0]
    hb = tm // FFN_HALO
    n_hb = S // FFN_HALO
    return pl.pallas_call(
        functools.partial(_ffn_kernel, tm=tm, d_ff=d_ff),
        grid=(B, S // tm),
        in_specs=[
            pl.BlockSpec((1, tm, D), lambda b, i: (b, i, 0)),
            pl.BlockSpec((1, FFN_HALO, D), lambda b, i: (b, jnp.maximum(i * hb - 1, 0), 0)),
            pl.BlockSpec((1, FFN_HALO, D), lambda b, i: (b, jnp.minimum((i + 1) * hb, n_hb - 1), 0)),
            pl.BlockSpec((1, 6, D), lambda b, i: (b, 0, 0)),
            _resident((1, D)),
            _resident(w_up.shape),
            _resident(ffn_w.shape),
            _resident((1, 2 * d_ff)),
            _resident(w_down.shape),
            _resident((1, D)),
        ],
        out_specs=pl.BlockSpec((1, tm, D), lambda b, i: (b, i, 0)),
        out_shape=jax.ShapeDtypeStruct((B, S, D), F32),
        scratch_shapes=[
            pltpu.VMEM((tm + 2 * FFN_HALO, D), BF16),
            pltpu.VMEM((tm, d_ff), BF16),
        ],
        compiler_params=pltpu.CompilerParams(
            dimension_semantics=("arbitrary", "arbitrary"), vmem_limit_bytes=VMEM_LIMIT_BYTES),
        name="conv_ffn_final_norm",
    )(x1, x1, x1, mod3, g2, w_up, ffn_w, ffn_b, w_down, g_final)


def kernel(x, c, ctx, c_ctx, w_mod, b_mod, g_norm1, w_in, rpb, conv_w, conv_b, ln_g, ln_b, w_out,
           g_norm2, w_up, ffn_conv_w, ffn_conv_b, w_down, g_final):
    B, S, D = x.shape
    depth = w_mod.shape[0]
    assert depth == 1, "the context-stream update of deeper stacks is not implemented"
    assert S % GRID_W == 0 and D_ATTN + conv_w.shape[-1] == D
    l = 0
    row = lambda a: a.reshape(1, -1)

    mod_rows = 8
    cc = jnp.concatenate([c, c_ctx[None], jnp.zeros((mod_rows - B - 1, D), c.dtype)], axis=0)
    mod_first = _modulation(cc, w_mod[l], row(b_mod[l]), 2 * D).reshape(mod_rows, 2, D)

    g1 = row(g_norm1[l])
    q, k, v, u, mod_rest = _input_projection(x, mod_first, g1, w_in[l], cc, w_mod[l], row(b_mod[l]),
                                             min(PROJ_TILE, S))
    mod3 = jnp.concatenate([mod_first, mod_rest.reshape(mod_rows, 4, D)], axis=1)
    k_c, v_c = _context_kv(ctx, mod_first, g1, w_in[l], ctx_row=B)

    y_na = _attention(q, k, v, k_c, v_c, rpb[l], rows_per_tile=ATTN_ROWS_PER_TILE)

    x1, w_up_b, w_down_b = _mix_out(
        y_na, u, x, mod3, conv_w[l], row(conv_b[l]), row(ln_g[l]), row(ln_b[l]),
        w_out[l].astype(BF16), w_up[l], w_down[l], min(MIX_TILE, S))

    return _conv_ffn(x1, mod3, row(g_norm2[l]), w_up_b, ffn_conv_w[l],
                     row(ffn_conv_b[l]), w_down_b, row(g_final), min(PROJ_TILE, S))
```

```python
import functools

import numpy as np
import jax
import jax.numpy as jnp
from jax import lax
from jax.experimental import pallas as pl
from jax.experimental.pallas import tpu as pltpu

F32 = jnp.float32
BF16 = jnp.bfloat16

GRID_W = 64
N_HEADS = 8
HEAD_DIM = 64
D_ATTN = N_HEADS * HEAD_DIM
CONV_WIDTH = 31
NA_ROWS = 8
NA_COLS = 16
FFN_CONV_WIDTH = 3
EPS = 1e-6
ATTN_SCALE = HEAD_DIM ** -0.5
LOG2_E = 1.4426950408889634

HEADS_PER_GROUP = 4
GROUP_W = HEADS_PER_GROUP * HEAD_DIM
N_BIAS_CLASSES = NA_ROWS
CONV_HALO = 16
FFN_HALO = 8
CONV_SPLIT = 16
MIX_ROW_BLOCK = 128
MIX_SLOTS = 2
FF_CHUNK = 256
FFN_DOWN_ROWS = 256
MOD_TILE = 2048
PROJ_TILE = 1024
MIX_TILE = 1024
ATTN_ROWS_PER_TILE = 32
ATTN_ROW_UNROLL = 16

VMEM_LIMIT_BYTES = 56 * 1024 * 1024


def _sigmoid(x):
    return 1.0 / (1.0 + jnp.exp2(x * (-LOG2_E)))


def _norm_modulate(x, gain, shift, scale):
    ms = jnp.mean(x * x, axis=-1, keepdims=True)
    return (x * lax.rsqrt(ms + EPS)) * (gain * (1.0 + scale)) + shift


def _resident(shape):
    nd = len(shape)
    return pl.BlockSpec(shape, lambda *_: (0,) * nd, pipeline_mode=pl.Buffered(1))


def _mod_kernel(c_ref, w_ref, b_ref, o_ref):
    c = c_ref[...]
    a = (c * _sigmoid(c)).astype(BF16)
    o_ref[...] = jnp.dot(a, w_ref[...].astype(BF16), preferred_element_type=F32) + b_ref[...]


def _modulation(cc, w_mod, b_mod, n):
    rows, d = cc.shape
    tn = min(MOD_TILE, n)
    return pl.pallas_call(
        _mod_kernel,
        grid=(n // tn,),
        in_specs=[
            pl.BlockSpec((rows, d), lambda j: (0, 0)),
            pl.BlockSpec((d, tn), lambda j: (0, j)),
            pl.BlockSpec((1, tn), lambda j: (0, j)),
        ],
        out_specs=pl.BlockSpec((rows, tn), lambda j: (0, j)),
        out_shape=jax.ShapeDtypeStruct((rows, n), F32),
        compiler_params=pltpu.CompilerParams(
            dimension_semantics=("arbitrary",), vmem_limit_bytes=VMEM_LIMIT_BYTES),
        name="modulation",
    )(cc, w_mod, b_mod)


def _inproj_kernel(x_ref, mod_ref, g_ref, w_ref, cc_ref, wm_ref, bm_ref, q_ref, k_ref, v_ref, u_ref,
                   modr_ref):
    _mod_kernel(cc_ref, wm_ref, bm_ref, modr_ref)
    h = _norm_modulate(x_ref[0], g_ref[...], mod_ref[0, 0:1, :], mod_ref[0, 1:2, :]).astype(BF16)

    def proj(lo):
        return jnp.dot(h, w_ref[:, lo:lo + D_ATTN].astype(BF16), preferred_element_type=F32)

    g = proj(4 * D_ATTN)
    a = proj(3 * D_ATTN)
    u_ref[0] = (a * _sigmoid(g)).astype(BF16)
    q_ref[0] = (proj(0) * (ATTN_SCALE * LOG2_E)).astype(BF16)
    k_ref[0] = proj(D_ATTN).astype(BF16)
    v_ref[0] = proj(2 * D_ATTN).astype(BF16)


def _ctx_kv_kernel(x_ref, mod_ref, g_ref, wk_ref, wv_ref, k_ref, v_ref):
    h = _norm_modulate(x_ref[0], g_ref[...], mod_ref[0, 0:1, :], mod_ref[0, 1:2, :]).astype(BF16)
    k_ref[0] = jnp.dot(h, wk_ref[...].astype(BF16), preferred_element_type=F32).astype(BF16)
    v_ref[0] = jnp.dot(h, wv_ref[...].astype(BF16), preferred_element_type=F32).astype(BF16)


def _input_projection(x, mod_first, g1, w_in, cc, w_mod, b_mod, tm):
    B, S, D = x.shape
    n_steps = S // tm
    n_first = mod_first.shape[1] * D
    n_rest = w_mod.shape[1] - n_first
    slab = n_rest // (B * n_steps)
    assert slab * B * n_steps == n_rest and slab % 128 == 0 and n_first % slab == 0
    out = jax.ShapeDtypeStruct((B, S, D_ATTN), BF16)
    tile = pl.BlockSpec((1, tm, D_ATTN), lambda b, i: (b, i, 0))
    return pl.pallas_call(
        _inproj_kernel,
        grid=(B, n_steps),
        in_specs=[
            pl.BlockSpec((1, tm, D), lambda b, i: (b, i, 0)),
            pl.BlockSpec((1, mod_first.shape[1], D), lambda b, i: (b, 0, 0)),
            _resident((1, D)),
            _resident(w_in.shape),
            _resident(cc.shape),
            pl.BlockSpec((D, slab), lambda b, i: (0, n_first // slab + b * n_steps + i)),
            pl.BlockSpec((1, slab), lambda b, i: (0, n_first // slab + b * n_steps + i)),
        ],
        out_specs=[tile, tile, tile, tile, pl.BlockSpec((cc.shape[0], slab), lambda b, i: (0, b * n_steps + i))],
        out_shape=[out, out, out, out, jax.ShapeDtypeStruct((cc.shape[0], n_rest), F32)],
        compiler_params=pltpu.CompilerParams(
            dimension_semantics=("arbitrary", "arbitrary"), vmem_limit_bytes=VMEM_LIMIT_BYTES),
        name="input_projection",
    )(x, mod_first, g1, w_in, cc, w_mod, b_mod)


def _context_kv(ctx, mod3, g1, w_in, ctx_row):
    B, L, D = ctx.shape
    out = jax.ShapeDtypeStruct((B, L, D_ATTN), BF16)
    tile = pl.BlockSpec((1, L, D_ATTN), lambda b: (b, 0, 0))
    return pl.pallas_call(
        _ctx_kv_kernel,
        grid=(B,),
        in_specs=[
            pl.BlockSpec((1, L, D), lambda b: (b, 0, 0)),
            pl.BlockSpec((1, mod3.shape[1], D), lambda b: (ctx_row, 0, 0)),
            _resident((1, D)),
            pl.BlockSpec((D, D_ATTN), lambda b: (0, 1), pipeline_mode=pl.Buffered(1)),
            pl.BlockSpec((D, D_ATTN), lambda b: (0, 2), pipeline_mode=pl.Buffered(1)),
        ],
        out_specs=[tile, tile],
        out_shape=[out, out],
        compiler_params=pltpu.CompilerParams(
            dimension_semantics=("arbitrary",), vmem_limit_bytes=VMEM_LIMIT_BYTES),
        name="context_kv",
    )(ctx, mod3, g1, w_in, w_in)


def _bias_operands(rpb):
    H, n_ro, n_co = rpb.shape
    half = n_co // 2
    period = 2 * GRID_W
    ro_pad = n_ro + 1
    z = jnp.concatenate(
        [rpb[..., half:], jnp.zeros((H, n_ro, period - n_co), rpb.dtype), rpb[..., :half]], axis=-1)
    z = jnp.pad(z.astype(F32), ((0, 0), (0, ro_pad - n_ro), (0, 0))).reshape(H * ro_pad, period)
    q = np.arange(GRID_W)
    start = np.clip(q - NA_COLS // 2, 0, GRID_W - NA_COLS)
    kc = np.arange(period) % GRID_W
    valid = (kc[None, :] >= start[:, None]) & (kc[None, :] < start[:, None] + NA_COLS)
    window = jnp.asarray(np.where(valid, 0.0, -np.inf), F32)
    return z, window


def _fill_bias_classes(z_ref, window_ref, bias_ref):
    period = z_ref.shape[-1]
    n_heads = bias_ref.shape[1]
    ro_pad = z_ref.shape[0] // n_heads
    n_ro = ro_pad - 1
    low_half = lax.broadcasted_iota(jnp.int32, (GRID_W, period), 1) < GRID_W
    for h in range(n_heads):
        toep = [pltpu.roll(jnp.broadcast_to(z_ref[h * ro_pad + ro:h * ro_pad + ro + 1, :], (GRID_W, period)),
                           0, 1, stride=1, stride_axis=0) for ro in range(n_ro)]
        pairs = [jnp.where(low_half, toep[ro], pltpu.roll(toep[ro + 1], GRID_W, 1)) * LOG2_E + window_ref[...]
                 for ro in range(n_ro - 1)]
        for c in range(N_BIAS_CLASSES):
            for jp in range(NA_ROWS // 2):
                bias_ref[c, h, :, jp * period:(jp + 1) * period] = pairs[c + 2 * jp]


def _attn_kernel(q_ref, k_ref, v_ref, kc_ref, vc_ref, z_ref, window_ref, o_ref, bias_ref, *,
                 rows_per_tile, n_rows):
    i = pl.program_id(1)

    @pl.when((pl.program_id(0) == 0) & (i == 0))
    def _():
        _fill_bias_classes(z_ref, window_ref, bias_ref)

    n_local = NA_ROWS * GRID_W
    lane_head = lax.broadcasted_iota(jnp.int32, (GRID_W, GROUP_W), 1) // HEAD_DIM
    nt = (((1,), (1,)), ((), ()))
    def row_body(r, carry):
        r_abs = i * rows_per_tile + r
        kr0 = jnp.clip(r_abs - NA_ROWS // 2, 0, n_rows - NA_ROWS)
        cls = kr0 - r_abs + (NA_ROWS - 1)
        key0 = pl.multiple_of(kr0 * GRID_W, GRID_W)
        q0 = pl.multiple_of(r * GRID_W, GRID_W)
        for g in range(N_HEADS // HEADS_PER_GROUP):
            ls = slice(g * GROUP_W, (g + 1) * GROUP_W)
            kc = kc_ref[0, :, ls]
            vc = vc_ref[0, :, ls]
            qg = q_ref[0, pl.ds(q0, GRID_W), ls]
            kl = k_ref[0, pl.ds(key0, n_local), ls]
            vl = v_ref[0, pl.ds(key0, n_local), ls]
            qs = jnp.concatenate(
                [jnp.where(lane_head == hh, qg, 0) for hh in range(HEADS_PER_GROUP)], axis=0)
            s_loc = lax.dot_general(qs, kl, nt, preferred_element_type=F32)
            s_ctx = lax.dot_general(qs, kc, nt, preferred_element_type=F32)
            bias = bias_ref[cls, g * HEADS_PER_GROUP:(g + 1) * HEADS_PER_GROUP]
            s_loc = s_loc + bias.reshape(HEADS_PER_GROUP * GRID_W, n_local)
            s = jnp.concatenate([s_loc, s_ctx], axis=1)
            p = jnp.exp2(s - jnp.max(s, axis=-1, keepdims=True))
            denom = jnp.sum(p, axis=-1, keepdims=True)
            pb = p.astype(BF16)
            o = (jnp.dot(pb[:, :n_local], vl, preferred_element_type=F32)
                 + jnp.dot(pb[:, n_local:], vc, preferred_element_type=F32))
            o = o * (1.0 / denom)
            y = o[0:GRID_W]
            for hh in range(1, HEADS_PER_GROUP):
                y = jnp.where(lane_head == hh, o[hh * GRID_W:(hh + 1) * GRID_W], y)
            o_ref[0, pl.ds(q0, GRID_W), ls] = y.astype(BF16)
        return carry

    lax.fori_loop(0, rows_per_tile, row_body, 0, unroll=ATTN_ROW_UNROLL)


def _attention(q, k, v, kc, vc, rpb, rows_per_tile):
    B, S, _ = q.shape
    L = kc.shape[1]
    n_rows = S // GRID_W
    tm = rows_per_tile * GRID_W
    z, window = _bias_operands(rpb)
    return pl.pallas_call(
        functools.partial(_attn_kernel, rows_per_tile=rows_per_tile, n_rows=n_rows),
        grid=(B, n_rows // rows_per_tile),
        in_specs=[
            pl.BlockSpec((1, tm, D_ATTN), lambda b, i: (b, i, 0)),
            pl.BlockSpec((1, S, D_ATTN), lambda b, i: (b, 0, 0)),
            pl.BlockSpec((1, S, D_ATTN), lambda b, i: (b, 0, 0)),
            pl.BlockSpec((1, L, D_ATTN), lambda b, i: (b, 0, 0)),
            pl.BlockSpec((1, L, D_ATTN), lambda b, i: (b, 0, 0)),
            _resident(z.shape),
            _resident(window.shape),
        ],
        out_specs=pl.BlockSpec((1, tm, D_ATTN), lambda b, i: (b, i, 0)),
        out_shape=jax.ShapeDtypeStruct((B, S, D_ATTN), BF16),
        scratch_shapes=[pltpu.VMEM((N_BIAS_CLASSES, rpb.shape[0], GRID_W, NA_ROWS * GRID_W), F32)],
        compiler_params=pltpu.CompilerParams(
            dimension_semantics=("arbitrary", "arbitrary"), vmem_limit_bytes=VMEM_LIMIT_BYTES),
        name="neighbourhood_attention",
    )(q, k, v, kc, vc, z, window)


def _shift_sum_matrix():
    span = MIX_ROW_BLOCK + CONV_SPLIT
    s_mat = np.zeros((MIX_ROW_BLOCK, CONV_SPLIT * span), np.float32)
    t = np.arange(MIX_ROW_BLOCK)
    for s in range(CONV_SPLIT):
        s_mat[t, s * span + t + s] = 1.0
    return jnp.asarray(s_mat, BF16)


def _mix_out_kernel(yna_ref, u_ref, up_ref, un_ref, x_ref, mod_ref, cw_ref, cb_ref, lg_ref, lb_ref,
                    shift_ref, w_ref, wu_ref, wd_ref, o_ref, wub_ref, wdb_ref, ext_ref, part_ref, ycv_ref,
                    *, tm):
    wub_ref[...] = wu_ref[...].astype(BF16)
    wdb_ref[...] = wd_ref[...].astype(BF16)
    i = pl.program_id(1)
    last = pl.num_programs(1) - 1
    zero = jnp.zeros((CONV_HALO, u_ref.shape[-1]), BF16)
    ext_ref[0:CONV_HALO] = jnp.where(i > 0, up_ref[0], zero)
    ext_ref[CONV_HALO:CONV_HALO + tm] = u_ref[0]
    ext_ref[CONV_HALO + tm:] = jnp.where(i < last, un_ref[0], zero)

    cb = cb_ref[...]
    lg = lg_ref[...]
    lb = lb_ref[...]
    d_attn = yna_ref.shape[-1]
    gate = mod_ref[0, 2:3, :]
    span = MIX_ROW_BLOCK + CONV_SPLIT
    n_tiles = span // CONV_SPLIT
    first_off = CONV_HALO - CONV_WIDTH // 2

    def partials(blk, slot):
        r0 = pl.multiple_of(blk * MIX_ROW_BLOCK, MIX_ROW_BLOCK)
        win = [ext_ref[pl.ds(r0 + v * CONV_SPLIT, CONV_SPLIT), :] for v in range(n_tiles + 1)]
        for s in range(CONV_SPLIT):
            taps = [(m, CONV_SPLIT * m + s - first_off) for m in range(2)]
            taps = [(m, j) for m, j in taps if 0 <= j < CONV_WIDTH]
            weights = [cw_ref[j] for _, j in taps]
            for v in range(n_tiles):
                part = weights[0] * win[v + taps[0][0]]
                for (m, _), w in zip(taps[1:], weights[1:]):
                    part = part + w * win[v + m]
                part_ref[slot, s * span + v * CONV_SPLIT:s * span + (v + 1) * CONV_SPLIT, :] = part

    def conv_norm(slot):
        acc = jnp.dot(shift_ref[...], part_ref[slot], preferred_element_type=F32) + cb
        mu = jnp.mean(acc, axis=-1, keepdims=True)
        xc = acc - mu
        var = jnp.mean(xc * xc, axis=-1, keepdims=True)
        y = xc * lax.rsqrt(var + EPS) * lg + lb
        ycv_ref[slot] = (y * _sigmoid(y)).astype(BF16)

    def project(blk, slot):
        rows = pl.ds(pl.multiple_of(blk * MIX_ROW_BLOCK, MIX_ROW_BLOCK), MIX_ROW_BLOCK)
        proj = (jnp.dot(yna_ref[0, rows, :], w_ref[0:d_attn, :], preferred_element_type=F32)
                + jnp.dot(ycv_ref[slot], w_ref[d_attn:, :], preferred_element_type=F32))
        o_ref[0, rows, :] = x_ref[0, rows, :] + gate * proj

    n_blocks = tm // MIX_ROW_BLOCK
    assert n_blocks >= 3
    partials(0, 0)
    partials(1, 1)
    conv_norm(0)
    conv_norm(1)
    partials(2, 0)
    for b in range(n_blocks):
        project(b, b % 2)
        if b + 2 < n_blocks:
            conv_norm(b % 2)
        if b + 3 < n_blocks:
            partials(b + 3, (b + 1) % 2)


def _mix_out(y_na, u, x, mod3, conv_w, conv_b, ln_g, ln_b, w_out, w_up, w_down, tm):
    B, S, D = x.shape
    d_conv = u.shape[-1]
    hb = tm // CONV_HALO
    n_hb = S // CONV_HALO
    cw_rep = jnp.broadcast_to(conv_w.astype(BF16)[:, None, :], (CONV_WIDTH, CONV_SPLIT, d_conv))
    shift = _shift_sum_matrix()
    n_steps = S // tm
    d_ff = w_down.shape[0]
    n_slabs = d_ff // FF_CHUNK
    assert B * n_steps >= n_slabs and w_up.shape[1] == 2 * d_ff
    slab = lambda b, i: jnp.minimum(b * n_steps + i, n_slabs - 1)
    return pl.pallas_call(
        functools.partial(_mix_out_kernel, tm=tm),
        grid=(B, S // tm),
        in_specs=[
            pl.BlockSpec((1, tm, D_ATTN), lambda b, i: (b, i, 0)),
            pl.BlockSpec((1, tm, d_conv), lambda b, i: (b, i, 0)),
            pl.BlockSpec((1, CONV_HALO, d_conv), lambda b, i: (b, jnp.maximum(i * hb - 1, 0), 0)),
            pl.BlockSpec((1, CONV_HALO, d_conv), lambda b, i: (b, jnp.minimum((i + 1) * hb, n_hb - 1), 0)),
            pl.BlockSpec((1, tm, D), lambda b, i: (b, i, 0)),
            pl.BlockSpec((1, 6, D), lambda b, i: (b, 0, 0)),
            _resident(cw_rep.shape),
            _resident((1, d_conv)),
            _resident((1, d_conv)),
            _resident((1, d_conv)),
            _resident(shift.shape),
            _resident(w_out.shape),
            pl.BlockSpec((D, 2 * FF_CHUNK), lambda b, i: (0, slab(b, i))),
            pl.BlockSpec((FF_CHUNK, D), lambda b, i: (slab(b, i), 0)),
        ],
        out_specs=[
            pl.BlockSpec((1, tm, D), lambda b, i: (b, i, 0)),
            pl.BlockSpec((D, 2 * FF_CHUNK), lambda b, i: (0, slab(b, i))),
            pl.BlockSpec((FF_CHUNK, D), lambda b, i: (slab(b, i), 0)),
        ],
        out_shape=[
            jax.ShapeDtypeStruct((B, S, D), F32),
            jax.ShapeDtypeStruct(w_up.shape, BF16),
            jax.ShapeDtypeStruct(w_down.shape, BF16),
        ],
        scratch_shapes=[
            pltpu.VMEM((tm + 2 * CONV_HALO, d_conv), BF16),
            pltpu.VMEM((MIX_SLOTS, shift.shape[1], d_conv), BF16),
            pltpu.VMEM((MIX_SLOTS, MIX_ROW_BLOCK, d_conv), BF16),
        ],
        compiler_params=pltpu.CompilerParams(
            dimension_semantics=("arbitrary", "arbitrary"), vmem_limit_bytes=VMEM_LIMIT_BYTES),
        name="conv_mixer_out_projection",
    )(y_na, u, u, u, x, mod3, cw_rep, conv_b, ln_g, ln_b, shift, w_out, w_up, w_down)


def _ffn_kernel(x_ref, xp_ref, xn_ref, mod_ref, g2_ref, wu_ref, fw_ref, fb_ref, wd_ref, gf_ref,
                o_ref, h_ref, act_ref, *, tm, d_ff):
    i = pl.program_id(1)
    last = pl.num_programs(1) - 1
    gain = g2_ref[...]
    shift = mod_ref[0, 3:4, :]
    scale = mod_ref[0, 4:5, :]
    hp = _norm_modulate(xp_ref[0], gain, shift, scale)
    hn = _norm_modulate(xn_ref[0], gain, shift, scale)
    h_ref[0:FFN_HALO] = jnp.where(i > 0, hp, 0.0).astype(BF16)
    h_ref[FFN_HALO:FFN_HALO + tm] = _norm_modulate(x_ref[0], gain, shift, scale).astype(BF16)
    h_ref[FFN_HALO + tm:] = jnp.where(i < last, hn, 0.0).astype(BF16)

    n_ext = tm + 2 * FFN_HALO

    def conv(lo):
        up = jnp.dot(h_ref[...], wu_ref[:, lo:lo + FF_CHUNK], preferred_element_type=F32)
        before = pltpu.roll(up, 1, axis=0)
        after = pltpu.roll(up, n_ext - 1, axis=0)
        mid = slice(FFN_HALO, FFN_HALO + tm)
        w = fw_ref[:, lo:lo + FF_CHUNK]
        return (w[0:1] * before[mid] + w[1:2] * up[mid] + w[2:3] * after[mid]
                + fb_ref[:, lo:lo + FF_CHUNK])

    for f in range(d_ff // FF_CHUNK):
        gate = conv(f * FF_CHUNK)
        val = conv(d_ff + f * FF_CHUNK)
        act_ref[:, f * FF_CHUNK:(f + 1) * FF_CHUNK] = (gate * _sigmoid(gate) * val).astype(BF16)

    for r in range(tm // FFN_DOWN_ROWS):
        rows = slice(r * FFN_DOWN_ROWS, (r + 1) * FFN_DOWN_ROWS)
        y = jnp.dot(act_ref[rows, :], wd_ref[...], preferred_element_type=F32)
        x2 = x_ref[0, rows, :] + mod_ref[0, 5:6, :] * y
        ms = jnp.mean(x2 * x2, axis=-1, keepdims=True)
        o_ref[0, rows, :] = x2 * lax.rsqrt(ms + EPS) * gf_ref[...]


def _conv_ffn(x1, mod3, g2, w_up, ffn_w, ffn_b, w_down, g_final, tm):
    B, S, D = x1.shape
    d_ff = w_down.shape[0]
    hb = tm // FFN_HALO
    n_hb = S // FFN_HALO
    return pl.pallas_call(
        functools.partial(_ffn_kernel, tm=tm, d_ff=d_ff),
        grid=(B, S // tm),
        in_specs=[
            pl.BlockSpec((1, tm, D), lambda b, i: (b, i, 0)),
            pl.BlockSpec((1, FFN_HALO, D), lambda b, i: (b, jnp.maximum(i * hb - 1, 0), 0)),
            pl.BlockSpec((1, FFN_HALO, D), lambda b, i: (b, jnp.minimum((i + 1) * hb, n_hb - 1), 0)),
            pl.BlockSpec((1, 6, D), lambda b, i: (b, 0, 0)),
            _resident((1, D)),
            _resident(w_up.shape),
            _resident(ffn_w.shape),
            _resident((1, 2 * d_ff)),
            _resident(w_down.shape),
            _resident((1, D)),
        ],
        out_specs=pl.BlockSpec((1, tm, D), lambda b, i: (b, i, 0)),
        out_shape=jax.ShapeDtypeStruct((B, S, D), F32),
        scratch_shapes=[
            pltpu.VMEM((tm + 2 * FFN_HALO, D), BF16),
            pltpu.VMEM((tm, d_ff), BF16),
        ],
        compiler_params=pltpu.CompilerParams(
            dimension_semantics=("arbitrary", "arbitrary"), vmem_limit_bytes=VMEM_LIMIT_BYTES),
        name="conv_ffn_final_norm",
    )(x1, x1, x1, mod3, g2, w_up, ffn_w, ffn_b, w_down, g_final)


def kernel(x, c, ctx, c_ctx, w_mod, b_mod, g_norm1, w_in, rpb, conv_w, conv_b, ln_g, ln_b, w_out,
           g_norm2, w_up, ffn_conv_w, ffn_conv_b, w_down, g_final):
    B, S, D = x.shape
    depth = w_mod.shape[0]
    assert depth == 1, "the context-stream update of deeper stacks is not implemented"
    assert S % GRID_W == 0 and D_ATTN + conv_w.shape[-1] == D
    l = 0
    row = lambda a: a.reshape(1, -1)

    mod_rows = 8
    cc = jnp.concatenate([c, c_ctx[None], jnp.zeros((mod_rows - B - 1, D), c.dtype)], axis=0)
    mod_first = _modulation(cc, w_mod[l], row(b_mod[l]), 2 * D).reshape(mod_rows, 2, D)

    g1 = row(g_norm1[l])
    q, k, v, u, mod_rest = _input_projection(x, mod_first, g1, w_in[l], cc, w_mod[l], row(b_mod[l]),
                                             min(PROJ_TILE, S))
    mod3 = jnp.concatenate([mod_first, mod_rest.reshape(mod_rows, 4, D)], axis=1)
    k_c, v_c = _context_kv(ctx, mod_first, g1, w_in[l], ctx_row=B)

    y_na = _attention(q, k, v, k_c, v_c, rpb[l], rows_per_tile=ATTN_ROWS_PER_TILE)

    x1, w_up_b, w_down_b = _mix_out(
        y_na, u, x, mod3, conv_w[l], row(conv_b[l]), row(ln_g[l]), row(ln_b[l]),
        w_out[l].astype(BF16), w_up[l], w_down[l], min(MIX_TILE, S))

    return _conv_ffn(x1, mod3, row(g_norm2[l]), w_up_b, ffn_conv_w[l],
                     row(ffn_conv_b[l]), w_down_b, row(g_final), min(PROJ_TILE, S))
```

```python
import functools

import numpy as np
import jax
import jax.numpy as jnp
from jax import lax
from jax.experimental import pallas as pl
from jax.experimental.pallas import tpu as pltpu

F32 = jnp.float32
BF16 = jnp.bfloat16

GRID_W = 64
N_HEADS = 8
HEAD_DIM = 64
D_ATTN = N_HEADS * HEAD_DIM
CONV_WIDTH = 31
NA_ROWS = 8
NA_COLS = 16
FFN_CONV_WIDTH = 3
EPS = 1e-6
ATTN_SCALE = HEAD_DIM ** -0.5
LOG2_E = 1.4426950408889634

HEADS_PER_GROUP = 4
GROUP_W = HEADS_PER_GROUP * HEAD_DIM
N_BIAS_CLASSES = NA_ROWS
CONV_HALO = 16
FFN_HALO = 8
CONV_SPLIT = 16
MIX_ROW_BLOCK = 128
MIX_SLOTS = 2
FF_CHUNK = 256
FFN_DOWN_ROWS = 256
MOD_TILE = 2048
CTX_BATCH_PER_STEP = 2
PROJ_TILE = 1024
MIX_TILE = 1024
ATTN_ROWS_PER_TILE = 32
ATTN_ROW_UNROLL = 16

VMEM_LIMIT_BYTES = 56 * 1024 * 1024


def _sigmoid(x):
    return 1.0 / (1.0 + jnp.exp(-x))


def _norm_modulate(x, gain, shift, scale):
    ms = jnp.mean(x * x, axis=-1, keepdims=True)
    return (x * lax.rsqrt(ms + EPS)) * (gain * (1.0 + scale)) + shift


def _resident(shape):
    nd = len(shape)
    return pl.BlockSpec(shape, lambda *_: (0,) * nd, pipeline_mode=pl.Buffered(1))


def _mod_kernel(c_ref, w_ref, b_ref, o_ref):
    c = c_ref[...]
    a = (c * _sigmoid(c)).astype(BF16)
    o_ref[...] = jnp.dot(a, w_ref[...].astype(BF16), preferred_element_type=F32) + b_ref[...]


def _modulation(cc, w_mod, b_mod, n):
    rows, d = cc.shape
    tn = min(MOD_TILE, n)
    return pl.pallas_call(
        _mod_kernel,
        grid=(n // tn,),
        in_specs=[
            pl.BlockSpec((rows, d), lambda j: (0, 0)),
            pl.BlockSpec((d, tn), lambda j: (0, j)),
            pl.BlockSpec((1, tn), lambda j: (0, j)),
        ],
        out_specs=pl.BlockSpec((rows, tn), lambda j: (0, j)),
        out_shape=jax.ShapeDtypeStruct((rows, n), F32),
        compiler_params=pltpu.CompilerParams(
            dimension_semantics=("arbitrary",), vmem_limit_bytes=VMEM_LIMIT_BYTES),
        name="modulation",
    )(cc, w_mod, b_mod)


def _inproj_kernel(x_ref, mod_ref, g_ref, w_ref, cc_ref, wm_ref, bm_ref, q_ref, k_ref, v_ref, u_ref,
                   modr_ref):
    _mod_kernel(cc_ref, wm_ref, bm_ref, modr_ref)
    h = _norm_modulate(x_ref[0], g_ref[...], mod_ref[0, 0:1, :], mod_ref[0, 1:2, :]).astype(BF16)

    def proj(lo):
        return jnp.dot(h, w_ref[:, lo:lo + D_ATTN].astype(BF16), preferred_element_type=F32)

    g = proj(4 * D_ATTN)
    a = proj(3 * D_ATTN)
    u_ref[0] = (a * _sigmoid(g)).astype(BF16)
    q_ref[0] = (proj(0) * (ATTN_SCALE * LOG2_E)).astype(BF16)
    k_ref[0] = proj(D_ATTN).astype(BF16)
    v_ref[0] = proj(2 * D_ATTN).astype(BF16)


def _ctx_kv_kernel(x_ref, mod_ref, g_ref, wk_ref, wv_ref, k_ref, v_ref):
    h = _norm_modulate(x_ref[0], g_ref[...], mod_ref[0, 0:1, :], mod_ref[0, 1:2, :]).astype(BF16)
    k_ref[0] = jnp.dot(h, wk_ref[...].astype(BF16), preferred_element_type=F32).astype(BF16)
    v_ref[0] = jnp.dot(h, wv_ref[...].astype(BF16), preferred_element_type=F32).astype(BF16)


def _input_projection(x, mod_first, g1, w_in, cc, w_mod, b_mod, tm):
    B, S, D = x.shape
    n_steps = S // tm
    n_first = mod_first.shape[1] * D
    n_rest = w_mod.shape[1] - n_first
    slab = n_rest // (B * n_steps)
    assert slab * B * n_steps == n_rest and slab % 128 == 0 and n_first % slab == 0
    out = jax.ShapeDtypeStruct((B, S, D_ATTN), BF16)
    tile = pl.BlockSpec((1, tm, D_ATTN), lambda b, i: (b, i, 0))
    return pl.pallas_call(
        _inproj_kernel,
        grid=(B, n_steps),
        in_specs=[
            pl.BlockSpec((1, tm, D), lambda b, i: (b, i, 0)),
            pl.BlockSpec((1, mod_first.shape[1], D), lambda b, i: (b, 0, 0)),
            _resident((1, D)),
            _resident(w_in.shape),
            _resident(cc.shape),
            pl.BlockSpec((D, slab), lambda b, i: (0, n_first // slab + b * n_steps + i)),
            pl.BlockSpec((1, slab), lambda b, i: (0, n_first // slab + b * n_steps + i)),
        ],
        out_specs=[tile, tile, tile, tile, pl.BlockSpec((cc.shape[0], slab), lambda b, i: (0, b * n_steps + i))],
        out_shape=[out, out, out, out, jax.ShapeDtypeStruct((cc.shape[0], n_rest), F32)],
        compiler_params=pltpu.CompilerParams(
            dimension_semantics=("arbitrary", "arbitrary"), vmem_limit_bytes=VMEM_LIMIT_BYTES),
        name="input_projection",
    )(x, mod_first, g1, w_in, cc, w_mod, b_mod)


def _context_kv(ctx, mod3, g1, w_in, ctx_row):
    B, L, D = ctx.shape
    nb = CTX_BATCH_PER_STEP if B % CTX_BATCH_PER_STEP == 0 else 1
    ctx = ctx.reshape(B // nb, nb * L, D)
    out = jax.ShapeDtypeStruct((B // nb, nb * L, D_ATTN), BF16)
    tile = pl.BlockSpec((1, nb * L, D_ATTN), lambda b: (b, 0, 0))
    k_c, v_c = pl.pallas_call(
        _ctx_kv_kernel,
        grid=(B // nb,),
        in_specs=[
            pl.BlockSpec((1, nb * L, D), lambda b: (b, 0, 0)),
            pl.BlockSpec((1, mod3.shape[1], D), lambda b: (ctx_row, 0, 0)),
            _resident((1, D)),
            pl.BlockSpec((D, D_ATTN), lambda b: (0, 1), pipeline_mode=pl.Buffered(1)),
            pl.BlockSpec((D, D_ATTN), lambda b: (0, 2), pipeline_mode=pl.Buffered(1)),
        ],
        out_specs=[tile, tile],
        out_shape=[out, out],
        compiler_params=pltpu.CompilerParams(
            dimension_semantics=("arbitrary",), vmem_limit_bytes=VMEM_LIMIT_BYTES),
        name="context_kv",
    )(ctx, mod3, g1, w_in, w_in)
    return k_c.reshape(B, L, D_ATTN), v_c.reshape(B, L, D_ATTN)


def _bias_operands(rpb):
    H, n_ro, n_co = rpb.shape
    half = n_co // 2
    period = 2 * GRID_W
    ro_pad = n_ro + 1
    z = jnp.concatenate(
        [rpb[..., half:], jnp.zeros((H, n_ro, period - n_co), rpb.dtype), rpb[..., :half]], axis=-1)
    z = jnp.pad(z.astype(F32), ((0, 0), (0, ro_pad - n_ro), (0, 0))).reshape(H * ro_pad, period)
    q = np.arange(GRID_W)
    start = np.clip(q - NA_COLS // 2, 0, GRID_W - NA_COLS)
    kc = np.arange(period) % GRID_W
    valid = (kc[None, :] >= start[:, None]) & (kc[None, :] < start[:, None] + NA_COLS)
    window = jnp.asarray(np.where(valid, 0.0, -np.inf), F32)
    return z, window


def _fill_bias_classes(z_ref, window_ref, bias_ref):
    period = z_ref.shape[-1]
    n_heads = bias_ref.shape[1]
    ro_pad = z_ref.shape[0] // n_heads
    n_ro = ro_pad - 1
    low_half = lax.broadcasted_iota(jnp.int32, (GRID_W, period), 1) < GRID_W
    for h in range(n_heads):
        toep = [pltpu.roll(jnp.broadcast_to(z_ref[h * ro_pad + ro:h * ro_pad + ro + 1, :], (GRID_W, period)),
                           0, 1, stride=1, stride_axis=0) for ro in range(n_ro)]
        pairs = [jnp.where(low_half, toep[ro], pltpu.roll(toep[ro + 1], GRID_W, 1)) * LOG2_E + window_ref[...]
                 for ro in range(n_ro - 1)]
        for c in range(N_BIAS_CLASSES):
            for jp in range(NA_ROWS // 2):
                bias_ref[c, h, :, jp * period:(jp + 1) * period] = pairs[c + 2 * jp]


def _attn_kernel(q_ref, k_ref, v_ref, kc_ref, vc_ref, z_ref, window_ref, o_ref, bias_ref, *,
                 rows_per_tile, n_rows):
    i = pl.program_id(1)

    @pl.when((pl.program_id(0) == 0) & (i == 0))
    def _():
        _fill_bias_classes(z_ref, window_ref, bias_ref)

    n_local = NA_ROWS * GRID_W
    lane_head = lax.broadcasted_iota(jnp.int32, (GRID_W, GROUP_W), 1) // HEAD_DIM
    nt = (((1,), (1,)), ((), ()))
    def row_body(r, carry):
        r_abs = i * rows_per_tile + r
        kr0 = jnp.clip(r_abs - NA_ROWS // 2, 0, n_rows - NA_ROWS)
        cls = kr0 - r_abs + (NA_ROWS - 1)
        key0 = pl.multiple_of(kr0 * GRID_W, GRID_W)
        q0 = pl.multiple_of(r * GRID_W, GRID_W)
        for g in range(N_HEADS // HEADS_PER_GROUP):
            ls = slice(g * GROUP_W, (g + 1) * GROUP_W)
            kc = kc_ref[0, :, ls]
            vc = vc_ref[0, :, ls]
            qg = q_ref[0, pl.ds(q0, GRID_W), ls]
            kl = k_ref[0, pl.ds(key0, n_local), ls]
            vl = v_ref[0, pl.ds(key0, n_local), ls]
            qs = jnp.concatenate(
                [jnp.where(lane_head == hh, qg, 0) for hh in range(HEADS_PER_GROUP)], axis=0)
            s_loc = lax.dot_general(qs, kl, nt, preferred_element_type=F32)
            s_ctx = lax.dot_general(qs, kc, nt, preferred_element_type=F32)
            bias = bias_ref[cls, g * HEADS_PER_GROUP:(g + 1) * HEADS_PER_GROUP]
            s_loc = s_loc + bias.reshape(HEADS_PER_GROUP * GRID_W, n_local)
            s = jnp.concatenate([s_loc, s_ctx], axis=1)
            p = jnp.exp2(s - jnp.max(s, axis=-1, keepdims=True))
            denom = jnp.sum(p, axis=-1, keepdims=True)
            pb = p.astype(BF16)
            o = (jnp.dot(pb[:, :n_local], vl, preferred_element_type=F32)
                 + jnp.dot(pb[:, n_local:], vc, preferred_element_type=F32))
            o = o * (1.0 / denom)
            y = o[0:GRID_W]
            for hh in range(1, HEADS_PER_GROUP):
                y = jnp.where(lane_head == hh, o[hh * GRID_W:(hh + 1) * GRID_W], y)
            o_ref[0, pl.ds(q0, GRID_W), ls] = y.astype(BF16)
        return carry

    lax.fori_loop(0, rows_per_tile, row_body, 0, unroll=ATTN_ROW_UNROLL)


def _attention(q, k, v, kc, vc, rpb, rows_per_tile):
    B, S, _ = q.shape
    L = kc.shape[1]
    n_rows = S // GRID_W
    tm = rows_per_tile * GRID_W
    z, window = _bias_operands(rpb)
    return pl.pallas_call(
        functools.partial(_attn_kernel, rows_per_tile=rows_per_tile, n_rows=n_rows),
        grid=(B, n_rows // rows_per_tile),
        in_specs=[
            pl.BlockSpec((1, tm, D_ATTN), lambda b, i: (b, i, 0)),
            pl.BlockSpec((1, S, D_ATTN), lambda b, i: (b, 0, 0)),
            pl.BlockSpec((1, S, D_ATTN), lambda b, i: (b, 0, 0)),
            pl.BlockSpec((1, L, D_ATTN), lambda b, i: (b, 0, 0)),
            pl.BlockSpec((1, L, D_ATTN), lambda b, i: (b, 0, 0)),
            _resident(z.shape),
            _resident(window.shape),
        ],
        out_specs=pl.BlockSpec((1, tm, D_ATTN), lambda b, i: (b, i, 0)),
        out_shape=jax.ShapeDtypeStruct((B, S, D_ATTN), BF16),
        scratch_shapes=[pltpu.VMEM((N_BIAS_CLASSES, rpb.shape[0], GRID_W, NA_ROWS * GRID_W), F32)],
        compiler_params=pltpu.CompilerParams(
            dimension_semantics=("arbitrary", "arbitrary"), vmem_limit_bytes=VMEM_LIMIT_BYTES),
        name="neighbourhood_attention",
    )(q, k, v, kc, vc, z, window)


def _shift_sum_matrix():
    span = MIX_ROW_BLOCK + CONV_SPLIT
    s_mat = np.zeros((MIX_ROW_BLOCK, CONV_SPLIT * span), np.float32)
    t = np.arange(MIX_ROW_BLOCK)
    for s in range(CONV_SPLIT):
        s_mat[t, s * span + t + s] = 1.0
    return jnp.asarray(s_mat, BF16)


def _mix_out_kernel(yna_ref, u_ref, up_ref, un_ref, x_ref, mod_ref, cw_ref, cb_ref, lg_ref, lb_ref,
                    shift_ref, w_ref, wu_ref, wd_ref, o_ref, wub_ref, wdb_ref, ext_ref, part_ref, ycv_ref,
                    *, tm):
    wub_ref[...] = wu_ref[...].astype(BF16)
    wdb_ref[...] = wd_ref[...].astype(BF16)
    i = pl.program_id(1)
    last = pl.num_programs(1) - 1
    zero = jnp.zeros((CONV_HALO, u_ref.shape[-1]), BF16)
    ext_ref[0:CONV_HALO] = jnp.where(i > 0, up_ref[0], zero)
    ext_ref[CONV_HALO:CONV_HALO + tm] = u_ref[0]
    ext_ref[CONV_HALO + tm:] = jnp.where(i < last, un_ref[0], zero)

    cb = cb_ref[...]
    lg = lg_ref[...]
    lb = lb_ref[...]
    d_attn = yna_ref.shape[-1]
    gate = mod_ref[0, 2:3, :]
    span = MIX_ROW_BLOCK + CONV_SPLIT
    n_tiles = span // CONV_SPLIT
    first_off = CONV_HALO - CONV_WIDTH // 2

    def partials(blk, slot):
        r0 = pl.multiple_of(blk * MIX_ROW_BLOCK, MIX_ROW_BLOCK)
        win = [ext_ref[pl.ds(r0 + v * CONV_SPLIT, CONV_SPLIT), :] for v in range(n_tiles + 1)]
        for s in range(CONV_SPLIT):
            taps = [(m, CONV_SPLIT * m + s - first_off) for m in range(2)]
            taps = [(m, j) for m, j in taps if 0 <= j < CONV_WIDTH]
            weights = [cw_ref[j] for _, j in taps]
            for v in range(n_tiles):
                part = weights[0] * win[v + taps[0][0]]
                for (m, _), w in zip(taps[1:], weights[1:]):
                    part = part + w * win[v + m]
                part_ref[slot, s * span + v * CONV_SPLIT:s * span + (v + 1) * CONV_SPLIT, :] = part

    def conv_norm(slot):
        acc = jnp.dot(shift_ref[...], part_ref[slot], preferred_element_type=F32) + cb
        mu = jnp.mean(acc, axis=-1, keepdims=True)
        xc = acc - mu
        var = jnp.mean(xc * xc, axis=-1, keepdims=True)
        y = xc * lax.rsqrt(var + EPS) * lg + lb
        ycv_ref[slot] = (y * _sigmoid(y)).astype(BF16)

    def project(blk, slot):
        rows = pl.ds(pl.multiple_of(blk * MIX_ROW_BLOCK, MIX_ROW_BLOCK), MIX_ROW_BLOCK)
        proj = (jnp.dot(yna_ref[0, rows, :], w_ref[0:d_attn, :], preferred_element_type=F32)
                + jnp.dot(ycv_ref[slot], w_ref[d_attn:, :], preferred_element_type=F32))
        o_ref[0, rows, :] = x_ref[0, rows, :] + gate * proj

    n_blocks = tm // MIX_ROW_BLOCK
    assert n_blocks >= 3
    partials(0, 0)
    partials(1, 1)
    conv_norm(0)
    conv_norm(1)
    partials(2, 0)
    for b in range(n_blocks):
        project(b, b % 2)
        if b + 2 < n_blocks:
            conv_norm(b % 2)
        if b + 3 < n_blocks:
            partials(b + 3, (b + 1) % 2)


def _mix_out(y_na, u, x, mod3, conv_w, conv_b, ln_g, ln_b, w_out, w_up, w_down, tm):
    B, S, D = x.shape
    d_conv = u.shape[-1]
    hb = tm // CONV_HALO
    n_hb = S // CONV_HALO
    cw_rep = jnp.broadcast_to(conv_w.astype(BF16)[:, None, :], (CONV_WIDTH, CONV_SPLIT, d_conv))
    shift = _shift_sum_matrix()
    n_steps = S // tm
    d_ff = w_down.shape[0]
    n_slabs = d_ff // FF_CHUNK
    assert B * n_steps >= n_slabs and w_up.shape[1] == 2 * d_ff
    slab = lambda b, i: jnp.minimum(b * n_steps + i, n_slabs - 1)
    return pl.pallas_call(
        functools.partial(_mix_out_kernel, tm=tm),
        grid=(B, S // tm),
        in_specs=[
            pl.BlockSpec((1, tm, D_ATTN), lambda b, i: (b, i, 0)),
            pl.BlockSpec((1, tm, d_conv), lambda b, i: (b, i, 0)),
            pl.BlockSpec((1, CONV_HALO, d_conv), lambda b, i: (b, jnp.maximum(i * hb - 1, 0), 0)),
            pl.BlockSpec((1, CONV_HALO, d_conv), lambda b, i: (b, jnp.minimum((i + 1) * hb, n_hb - 1), 0)),
            pl.BlockSpec((1, tm, D), lambda b, i: (b, i, 0)),
            pl.BlockSpec((1, 6, D), lambda b, i: (b, 0, 0)),
            _resident(cw_rep.shape),
            _resident((1, d_conv)),
            _resident((1, d_conv)),
            _resident((1, d_conv)),
            _resident(shift.shape),
            _resident(w_out.shape),
            pl.BlockSpec((D, 2 * FF_CHUNK), lambda b, i: (0, slab(b, i))),
            pl.BlockSpec((FF_CHUNK, D), lambda b, i: (slab(b, i), 0)),
        ],
        out_specs=[
            pl.BlockSpec((1, tm, D), lambda b, i: (b, i, 0)),
            pl.BlockSpec((D, 2 * FF_CHUNK), lambda b, i: (0, slab(b, i))),
            pl.BlockSpec((FF_CHUNK, D), lambda b, i: (slab(b, i), 0)),
        ],
        out_shape=[
            jax.ShapeDtypeStruct((B, S, D), F32),
            jax.ShapeDtypeStruct(w_up.shape, BF16),
            jax.ShapeDtypeStruct(w_down.shape, BF16),
        ],
        scratch_shapes=[
            pltpu.VMEM((tm + 2 * CONV_HALO, d_conv), BF16),
            pltpu.VMEM((MIX_SLOTS, shift.shape[1], d_conv), BF16),
            pltpu.VMEM((MIX_SLOTS, MIX_ROW_BLOCK, d_conv), BF16),
        ],
        compiler_params=pltpu.CompilerParams(
            dimension_semantics=("arbitrary", "arbitrary"), vmem_limit_bytes=VMEM_LIMIT_BYTES),
        name="conv_mixer_out_projection",
    )(y_na, u, u, u, x, mod3, cw_rep, conv_b, ln_g, ln_b, shift, w_out, w_up, w_down)


def _ffn_kernel(x_ref, xp_ref, xn_ref, mod_ref, g2_ref, wu_ref, fw_ref, fb_ref, wd_ref, gf_ref,
                o_ref, h_ref, act_ref, *, tm, d_ff):
    i = pl.program_id(1)
    last = pl.num_programs(1) - 1
    gain = g2_ref[...]
    shift = mod_ref[0, 3:4, :]
    scale = mod_ref[0, 4:5, :]
    hp = _norm_modulate(xp_ref[0], gain, shift, scale)
    hn = _norm_modulate(xn_ref[0], gain, shift, scale)
    h_ref[0:FFN_HALO] = jnp.where(i > 0, hp, 0.0).astype(BF16)
    h_ref[FFN_HALO:FFN_HALO + tm] = _norm_modulate(x_ref[0], gain, shift, scale).astype(BF16)
    h_ref[FFN_HALO + tm:] = jnp.where(i < last, hn, 0.0).astype(BF16)

    n_ext = tm + 2 * FFN_HALO

    def conv(lo):
        up = jnp.dot(h_ref[...], wu_ref[:, lo:lo + FF_CHUNK], preferred_element_type=F32)
        before = pltpu.roll(up, 1, axis=0)
        after = pltpu.roll(up, n_ext - 1, axis=0)
        mid = slice(FFN_HALO, FFN_HALO + tm)
        w = fw_ref[:, lo:lo + FF_CHUNK]
        return (w[0:1] * before[mid] + w[1:2] * up[mid] + w[2:3] * after[mid]
                + fb_ref[:, lo:lo + FF_CHUNK])

    for f in range(d_ff // FF_CHUNK):
        gate = conv(f * FF_CHUNK)
        val = conv(d_ff + f * FF_CHUNK)
        act_ref[:, f * FF_CHUNK:(f + 1) * FF_CHUNK] = (gate * _sigmoid(gate) * val).astype(BF16)

    for r in range(tm // FFN_DOWN_ROWS):
        rows = slice(r * FFN_DOWN_ROWS, (r + 1) * FFN_DOWN_ROWS)
        y = jnp.dot(act_ref[rows, :], wd_ref[...], preferred_element_type=F32)
        x2 = x_ref[0, rows, :] + mod_ref[0, 5:6, :] * y
        ms = jnp.mean(x2 * x2, axis=-1, keepdims=True)
        o_ref[0, rows, :] = x2 * lax.rsqrt(ms + EPS) * gf_ref[...]


def _conv_ffn(x1, mod3, g2, w_up, ffn_w, ffn_b, w_down, g_final, tm):
    B, S, D = x1.shape
    d_ff = w_down.shape[0]
    hb = tm // FFN_HALO
    n_hb = S // FFN_HALO
    return pl.pallas_call(
        functools.partial(_ffn_kernel, tm=tm, d_ff=d_ff),
        grid=(B, S // tm),
        in_specs=[
            pl.BlockSpec((1, tm, D), lambda b, i: (b, i, 0)),
            pl.BlockSpec((1, FFN_HALO, D), lambda b, i: (b, jnp.maximum(i * hb - 1, 0), 0)),
            pl.BlockSpec((1, FFN_HALO, D), lambda b, i: (b, jnp.minimum((i + 1) * hb, n_hb - 1), 0)),
            pl.BlockSpec((1, 6, D), lambda b, i: (b, 0, 0)),
            _resident((1, D)),
            _resident(w_up.shape),
            _resident(ffn_w.shape),
            _resident((1, 2 * d_ff)),
            _resident(w_down.shape),
            _resident((1, D)),
        ],
        out_specs=pl.BlockSpec((1, tm, D), lambda b, i: (b, i, 0)),
        out_shape=jax.ShapeDtypeStruct((B, S, D), F32),
        scratch_shapes=[
            pltpu.VMEM((tm + 2 * FFN_HALO, D), BF16),
            pltpu.VMEM((tm, d_ff), BF16),
        ],
        compiler_params=pltpu.CompilerParams(
            dimension_semantics=("arbitrary", "arbitrary"), vmem_limit_bytes=VMEM_LIMIT_BYTES),
        name="conv_ffn_final_norm",
    )(x1, x1, x1, mod3, g2, w_up, ffn_w, ffn_b, w_down, g_final)


def kernel(x, c, ctx, c_ctx, w_mod, b_mod, g_norm1, w_in, rpb, conv_w, conv_b, ln_g, ln_b, w_out,
           g_norm2, w_up, ffn_conv_w, ffn_conv_b, w_down, g_final):
    B, S, D = x.shape
    depth = w_mod.shape[0]
    assert depth == 1, "the context-stream update of deeper stacks is not implemented"
    assert S % GRID_W == 0 and D_ATTN + conv_w.shape[-1] == D
    l = 0
    row = lambda a: a.reshape(1, -1)

    mod_rows = 8
    cc = jnp.concatenate([c, c_ctx[None], jnp.zeros((mod_rows - B - 1, D), c.dtype)], axis=0)
    mod_first = _modulation(cc, w_mod[l], row(b_mod[l]), 2 * D).reshape(mod_rows, 2, D)

    g1 = row(g_norm1[l])
    q, k, v, u, mod_rest = _input_projection(x, mod_first, g1, w_in[l], cc, w_mod[l], row(b_mod[l]),
                                             min(PROJ_TILE, S))
    mod3 = jnp.concatenate([mod_first, mod_rest.reshape(mod_rows, 4, D)], axis=1)
    k_c, v_c = _context_kv(ctx, mod_first, g1, w_in[l], ctx_row=B)

    y_na = _attention(q, k, v, k_c, v_c, rpb[l], rows_per_tile=ATTN_ROWS_PER_TILE)

    x1, w_up_b, w_down_b = _mix_out(
        y_na, u, x, mod3, conv_w[l], row(conv_b[l]), row(ln_g[l]), row(ln_b[l]),
        w_out[l].astype(BF16), w_up[l], w_down[l], min(MIX_TILE, S))

    return _conv_ffn(x1, mod3, row(g_norm2[l]), w_up_b, ffn_conv_w[l],
                     row(ffn_conv_b[l]), w_down_b, row(g_final), min(PROJ_TILE, S))
```

```python
import functools

import numpy as np
import jax
import jax.numpy as jnp
from jax import lax
from jax.experimental import pallas as pl
from jax.experimental.pallas import tpu as pltpu

F32 = jnp.float32
BF16 = jnp.bfloat16

GRID_W = 64
N_HEADS = 8
HEAD_DIM = 64
D_ATTN = N_HEADS * HEAD_DIM
CONV_WIDTH = 31
NA_ROWS = 8
NA_COLS = 16
FFN_CONV_WIDTH = 3
EPS = 1e-6
ATTN_SCALE = HEAD_DIM ** -0.5
LOG2_E = 1.4426950408889634

HEADS_PER_GROUP = 4
GROUP_W = HEADS_PER_GROUP * HEAD_DIM
N_BIAS_CLASSES = NA_ROWS
CONV_HALO = 16
FFN_HALO = 8
CONV_SPLIT = 16
MIX_ROW_BLOCK = 128
MIX_SLOTS = 2
FF_CHUNK = 256
FFN_DOWN_ROWS = 256
MOD_TILE = 2048
CTX_BATCH_PER_STEP = 2
PROJ_TILE = 1024
MIX_TILE = 1024
ATTN_ROWS_PER_TILE = 32
ATTN_ROW_UNROLL = 16

VMEM_LIMIT_BYTES = 56 * 1024 * 1024


def _sigmoid(x):
    return 1.0 / (1.0 + jnp.exp(-x))


def _norm_modulate(x, gain, shift, scale):
    ms = jnp.mean(x * x, axis=-1, keepdims=True)
    return (x * lax.rsqrt(ms + EPS)) * (gain * (1.0 + scale)) + shift


def _resident(shape):
    nd = len(shape)
    return pl.BlockSpec(shape, lambda *_: (0,) * nd, pipeline_mode=pl.Buffered(1))


def _mod_kernel(c_ref, w_ref, b_ref, o_ref):
    c = c_ref[...]
    a = (c * _sigmoid(c)).astype(BF16)
    o_ref[...] = jnp.dot(a, w_ref[...].astype(BF16), preferred_element_type=F32) + b_ref[...]


def _modulation(cc, w_mod, b_mod, n):
    rows, d = cc.shape
    tn = min(MOD_TILE, n)
    return pl.pallas_call(
        _mod_kernel,
        grid=(n // tn,),
        in_specs=[
            pl.BlockSpec((rows, d), lambda j: (0, 0)),
            pl.BlockSpec((d, tn), lambda j: (0, j)),
            pl.BlockSpec((1, tn), lambda j: (0, j)),
        ],
        out_specs=pl.BlockSpec((rows, tn), lambda j: (0, j)),
        out_shape=jax.ShapeDtypeStruct((rows, n), F32),
        compiler_params=pltpu.CompilerParams(
            dimension_semantics=("arbitrary",), vmem_limit_bytes=VMEM_LIMIT_BYTES),
        name="modulation",
    )(cc, w_mod, b_mod)


def _inproj_kernel(x_ref, mod_ref, g_ref, w_ref, cc_ref, wm_ref, bm_ref, q_ref, k_ref, v_ref, u_ref,
                   modr_ref):
    _mod_kernel(cc_ref, wm_ref, bm_ref, modr_ref)
    h = _norm_modulate(x_ref[0], g_ref[...], mod_ref[0, 0:1, :], mod_ref[0, 1:2, :]).astype(BF16)

    def proj(lo):
        return jnp.dot(h, w_ref[:, lo:lo + D_ATTN].astype(BF16), preferred_element_type=F32)

    g = proj(4 * D_ATTN)
    a = proj(3 * D_ATTN)
    u_ref[0] = (a * _sigmoid(g)).astype(BF16)
    q_ref[0] = (proj(0) * (ATTN_SCALE * LOG2_E)).astype(BF16)
    k_ref[0] = proj(D_ATTN).astype(BF16)
    v_ref[0] = proj(2 * D_ATTN).astype(BF16)


def _ctx_kv_kernel(x_ref, mod_ref, g_ref, wk_ref, wv_ref, k_ref, v_ref):
    h = _norm_modulate(x_ref[0], g_ref[...], mod_ref[0, 0:1, :], mod_ref[0, 1:2, :]).astype(BF16)
    k_ref[0] = jnp.dot(h, wk_ref[...].astype(BF16), preferred_element_type=F32).astype(BF16)
    v_ref[0] = jnp.dot(h, wv_ref[...].astype(BF16), preferred_element_type=F32).astype(BF16)


def _conditioning_kernel(cc_ref, wm_ref, bm_ref, x_ref, g_ref, wk_ref, wv_ref, mod_ref, k_ref, v_ref, *,
                         ctx_row):
    step = pl.program_id(0)
    d = x_ref.shape[-1]

    @pl.when(step == 0)
    def _():
        _mod_kernel(cc_ref, wm_ref, bm_ref, mod_ref)

    @pl.when(step > 0)
    def _():
        shift = mod_ref[ctx_row:ctx_row + 1, 0:d]
        scale = mod_ref[ctx_row:ctx_row + 1, d:2 * d]
        h = _norm_modulate(x_ref[0], g_ref[...], shift, scale).astype(BF16)
        k_ref[0] = jnp.dot(h, wk_ref[...].astype(BF16), preferred_element_type=F32).astype(BF16)
        v_ref[0] = jnp.dot(h, wv_ref[...].astype(BF16), preferred_element_type=F32).astype(BF16)


def _conditioning(cc, w_mod, b_mod, ctx, g1, w_in, ctx_row):
    B, L, D = ctx.shape
    rows = cc.shape[0]
    n_first = 2 * D
    nb = CTX_BATCH_PER_STEP if B % CTX_BATCH_PER_STEP == 0 else 1
    ctx = ctx.reshape(B // nb, nb * L, D)
    kv = jax.ShapeDtypeStruct((B // nb, nb * L, D_ATTN), BF16)
    ctx_blk = lambda s: (jnp.maximum(s - 1, 0), 0, 0)
    const = lambda shape, idx: pl.BlockSpec(shape, lambda s: idx, pipeline_mode=pl.Buffered(1))
    mod_first, k_c, v_c = pl.pallas_call(
        functools.partial(_conditioning_kernel, ctx_row=ctx_row),
        grid=(1 + B // nb,),
        in_specs=[
            _resident(cc.shape),
            const((D, n_first), (0, 0)),
            const((1, n_first), (0, 0)),
            pl.BlockSpec((1, nb * L, D), ctx_blk),
            _resident((1, D)),
            const((D, D_ATTN), (0, 1)),
            const((D, D_ATTN), (0, 2)),
        ],
        out_specs=[
            pl.BlockSpec((rows, n_first), lambda s: (0, 0)),
            pl.BlockSpec((1, nb * L, D_ATTN), ctx_blk),
            pl.BlockSpec((1, nb * L, D_ATTN), ctx_blk),
        ],
        out_shape=[jax.ShapeDtypeStruct((rows, n_first), F32), kv, kv],
        compiler_params=pltpu.CompilerParams(
            dimension_semantics=("arbitrary",), vmem_limit_bytes=VMEM_LIMIT_BYTES),
        name="conditioning",
    )(cc, w_mod, b_mod, ctx, g1, w_in, w_in)
    return mod_first.reshape(rows, 2, D), k_c.reshape(B, L, D_ATTN), v_c.reshape(B, L, D_ATTN)


def _input_projection(x, mod_first, g1, w_in, cc, w_mod, b_mod, tm):
    B, S, D = x.shape
    n_steps = S // tm
    n_first = mod_first.shape[1] * D
    n_rest = w_mod.shape[1] - n_first
    slab = n_rest // (B * n_steps)
    assert slab * B * n_steps == n_rest and slab % 128 == 0 and n_first % slab == 0
    out = jax.ShapeDtypeStruct((B, S, D_ATTN), BF16)
    tile = pl.BlockSpec((1, tm, D_ATTN), lambda b, i: (b, i, 0))
    return pl.pallas_call(
        _inproj_kernel,
        grid=(B, n_steps),
        in_specs=[
            pl.BlockSpec((1, tm, D), lambda b, i: (b, i, 0)),
            pl.BlockSpec((1, mod_first.shape[1], D), lambda b, i: (b, 0, 0)),
            _resident((1, D)),
            _resident(w_in.shape),
            _resident(cc.shape),
            pl.BlockSpec((D, slab), lambda b, i: (0, n_first // slab + b * n_steps + i)),
            pl.BlockSpec((1, slab), lambda b, i: (0, n_first // slab + b * n_steps + i)),
        ],
        out_specs=[tile, tile, tile, tile, pl.BlockSpec((cc.shape[0], slab), lambda b, i: (0, b * n_steps + i))],
        out_shape=[out, out, out, out, jax.ShapeDtypeStruct((cc.shape[0], n_rest), F32)],
        compiler_params=pltpu.CompilerParams(
            dimension_semantics=("arbitrary", "arbitrary"), vmem_limit_bytes=VMEM_LIMIT_BYTES),
        name="input_projection",
    )(x, mod_first, g1, w_in, cc, w_mod, b_mod)


def _context_kv(ctx, mod3, g1, w_in, ctx_row):
    B, L, D = ctx.shape
    nb = CTX_BATCH_PER_STEP if B % CTX_BATCH_PER_STEP == 0 else 1
    ctx = ctx.reshape(B // nb, nb * L, D)
    out = jax.ShapeDtypeStruct((B // nb, nb * L, D_ATTN), BF16)
    tile = pl.BlockSpec((1, nb * L, D_ATTN), lambda b: (b, 0, 0))
    k_c, v_c = pl.pallas_call(
        _ctx_kv_kernel,
        grid=(B // nb,),
        in_specs=[
            pl.BlockSpec((1, nb * L, D), lambda b: (b, 0, 0)),
            pl.BlockSpec((1, mod3.shape[1], D), lambda b: (ctx_row, 0, 0)),
            _resident((1, D)),
            pl.BlockSpec((D, D_ATTN), lambda b: (0, 1), pipeline_mode=pl.Buffered(1)),
            pl.BlockSpec((D, D_ATTN), lambda b: (0, 2), pipeline_mode=pl.Buffered(1)),
        ],
        out_specs=[tile, tile],
        out_shape=[out, out],
        compiler_params=pltpu.CompilerParams(
            dimension_semantics=("arbitrary",), vmem_limit_bytes=VMEM_LIMIT_BYTES),
        name="context_kv",
    )(ctx, mod3, g1, w_in, w_in)
    return k_c.reshape(B, L, D_ATTN), v_c.reshape(B, L, D_ATTN)


def _bias_operands(rpb):
    H, n_ro, n_co = rpb.shape
    half = n_co // 2
    period = 2 * GRID_W
    ro_pad = n_ro + 1
    z = jnp.concatenate(
        [rpb[..., half:], jnp.zeros((H, n_ro, period - n_co), rpb.dtype), rpb[..., :half]], axis=-1)
    z = jnp.pad(z.astype(F32), ((0, 0), (0, ro_pad - n_ro), (0, 0))).reshape(H * ro_pad, period)
    q = np.arange(GRID_W)
    start = np.clip(q - NA_COLS // 2, 0, GRID_W - NA_COLS)
    kc = np.arange(period) % GRID_W
    valid = (kc[None, :] >= start[:, None]) & (kc[None, :] < start[:, None] + NA_COLS)
    window = jnp.asarray(np.where(valid, 0.0, -np.inf), F32)
    return z, window


def _fill_bias_classes(z_ref, window_ref, bias_ref):
    period = z_ref.shape[-1]
    n_heads = bias_ref.shape[1]
    ro_pad = z_ref.shape[0] // n_heads
    n_ro = ro_pad - 1
    low_half = lax.broadcasted_iota(jnp.int32, (GRID_W, period), 1) < GRID_W
    for h in range(n_heads):
        toep = [pltpu.roll(jnp.broadcast_to(z_ref[h * ro_pad + ro:h * ro_pad + ro + 1, :], (GRID_W, period)),
                           0, 1, stride=1, stride_axis=0) for ro in range(n_ro)]
        pairs = [jnp.where(low_half, toep[ro], pltpu.roll(toep[ro + 1], GRID_W, 1)) * LOG2_E + window_ref[...]
                 for ro in range(n_ro - 1)]
        for c in range(N_BIAS_CLASSES):
            for jp in range(NA_ROWS // 2):
                bias_ref[c, h, :, jp * period:(jp + 1) * period] = pairs[c + 2 * jp]


def _attn_kernel(q_ref, k_ref, v_ref, kc_ref, vc_ref, z_ref, window_ref, o_ref, bias_ref, *,
                 rows_per_tile, n_rows):
    i = pl.program_id(1)

    @pl.when((pl.program_id(0) == 0) & (i == 0))
    def _():
        _fill_bias_classes(z_ref, window_ref, bias_ref)

    n_local = NA_ROWS * GRID_W
    lane_head = lax.broadcasted_iota(jnp.int32, (GRID_W, GROUP_W), 1) // HEAD_DIM
    nt = (((1,), (1,)), ((), ()))
    def row_body(r, carry):
        r_abs = i * rows_per_tile + r
        kr0 = jnp.clip(r_abs - NA_ROWS // 2, 0, n_rows - NA_ROWS)
        cls = kr0 - r_abs + (NA_ROWS - 1)
        key0 = pl.multiple_of(kr0 * GRID_W, GRID_W)
        q0 = pl.multiple_of(r * GRID_W, GRID_W)
        for g in range(N_HEADS // HEADS_PER_GROUP):
            ls = slice(g * GROUP_W, (g + 1) * GROUP_W)
            kc = kc_ref[0, :, ls]
            vc = vc_ref[0, :, ls]
            qg = q_ref[0, pl.ds(q0, GRID_W), ls]
            kl = k_ref[0, pl.ds(key0, n_local), ls]
            vl = v_ref[0, pl.ds(key0, n_local), ls]
            qs = jnp.concatenate(
                [jnp.where(lane_head == hh, qg, 0) for hh in range(HEADS_PER_GROUP)], axis=0)
            s_loc = lax.dot_general(qs, kl, nt, preferred_element_type=F32)
            s_ctx = lax.dot_general(qs, kc, nt, preferred_element_type=F32)
            bias = bias_ref[cls, g * HEADS_PER_GROUP:(g + 1) * HEADS_PER_GROUP]
            s_loc = s_loc + bias.reshape(HEADS_PER_GROUP * GRID_W, n_local)
            s = jnp.concatenate([s_loc, s_ctx], axis=1)
            p = jnp.exp2(s - jnp.max(s, axis=-1, keepdims=True))
            denom = jnp.sum(p, axis=-1, keepdims=True)
            pb = p.astype(BF16)
            o = (jnp.dot(pb[:, :n_local], vl, preferred_element_type=F32)
                 + jnp.dot(pb[:, n_local:], vc, preferred_element_type=F32))
            o = o * (1.0 / denom)
            y = o[0:GRID_W]
            for hh in range(1, HEADS_PER_GROUP):
                y = jnp.where(lane_head == hh, o[hh * GRID_W:(hh + 1) * GRID_W], y)
            o_ref[0, pl.ds(q0, GRID_W), ls] = y.astype(BF16)
        return carry

    lax.fori_loop(0, rows_per_tile, row_body, 0, unroll=ATTN_ROW_UNROLL)


def _attention(q, k, v, kc, vc, rpb, rows_per_tile):
    B, S, _ = q.shape
    L = kc.shape[1]
    n_rows = S // GRID_W
    tm = rows_per_tile * GRID_W
    z, window = _bias_operands(rpb)
    return pl.pallas_call(
        functools.partial(_attn_kernel, rows_per_tile=rows_per_tile, n_rows=n_rows),
        grid=(B, n_rows // rows_per_tile),
        in_specs=[
            pl.BlockSpec((1, tm, D_ATTN), lambda b, i: (b, i, 0)),
            pl.BlockSpec((1, S, D_ATTN), lambda b, i: (b, 0, 0)),
            pl.BlockSpec((1, S, D_ATTN), lambda b, i: (b, 0, 0)),
            pl.BlockSpec((1, L, D_ATTN), lambda b, i: (b, 0, 0)),
            pl.BlockSpec((1, L, D_ATTN), lambda b, i: (b, 0, 0)),
            _resident(z.shape),
            _resident(window.shape),
        ],
        out_specs=pl.BlockSpec((1, tm, D_ATTN), lambda b, i: (b, i, 0)),
        out_shape=jax.ShapeDtypeStruct((B, S, D_ATTN), BF16),
        scratch_shapes=[pltpu.VMEM((N_BIAS_CLASSES, rpb.shape[0], GRID_W, NA_ROWS * GRID_W), F32)],
        compiler_params=pltpu.CompilerParams(
            dimension_semantics=("arbitrary", "arbitrary"), vmem_limit_bytes=VMEM_LIMIT_BYTES),
        name="neighbourhood_attention",
    )(q, k, v, kc, vc, z, window)


def _shift_sum_matrix():
    span = MIX_ROW_BLOCK + CONV_SPLIT
    s_mat = np.zeros((MIX_ROW_BLOCK, CONV_SPLIT * span), np.float32)
    t = np.arange(MIX_ROW_BLOCK)
    for s in range(CONV_SPLIT):
        s_mat[t, s * span + t + s] = 1.0
    return jnp.asarray(s_mat, BF16)


def _mix_out_kernel(yna_ref, u_ref, up_ref, un_ref, x_ref, mod_ref, cw_ref, cb_ref, lg_ref, lb_ref,
                    shift_ref, w_ref, wu_ref, wd_ref, o_ref, wub_ref, wdb_ref, ext_ref, part_ref, ycv_ref,
                    *, tm):
    wub_ref[...] = wu_ref[...].astype(BF16)
    wdb_ref[...] = wd_ref[...].astype(BF16)
    i = pl.program_id(1)
    last = pl.num_programs(1) - 1
    zero = jnp.zeros((CONV_HALO, u_ref.shape[-1]), BF16)
    ext_ref[0:CONV_HALO] = jnp.where(i > 0, up_ref[0], zero)
    ext_ref[CONV_HALO:CONV_HALO + tm] = u_ref[0]
    ext_ref[CONV_HALO + tm:] = jnp.where(i < last, un_ref[0], zero)

    cb = cb_ref[...]
    lg = lg_ref[...]
    lb = lb_ref[...]
    d_attn = yna_ref.shape[-1]
    gate = mod_ref[0, 2:3, :]
    span = MIX_ROW_BLOCK + CONV_SPLIT
    n_tiles = span // CONV_SPLIT
    first_off = CONV_HALO - CONV_WIDTH // 2

    def partials(blk, slot):
        r0 = pl.multiple_of(blk * MIX_ROW_BLOCK, MIX_ROW_BLOCK)
        win = [ext_ref[pl.ds(r0 + v * CONV_SPLIT, CONV_SPLIT), :] for v in range(n_tiles + 1)]
        for s in range(CONV_SPLIT):
            taps = [(m, CONV_SPLIT * m + s - first_off) for m in range(2)]
            taps = [(m, j) for m, j in taps if 0 <= j < CONV_WIDTH]
            weights = [cw_ref[j] for _, j in taps]
            for v in range(n_tiles):
                part = weights[0] * win[v + taps[0][0]]
                for (m, _), w in zip(taps[1:], weights[1:]):
                    part = part + w * win[v + m]
                part_ref[slot, s * span + v * CONV_SPLIT:s * span + (v + 1) * CONV_SPLIT, :] = part

    def conv_norm(slot):
        acc = jnp.dot(shift_ref[...], part_ref[slot], preferred_element_type=F32) + cb
        mu = jnp.mean(acc, axis=-1, keepdims=True)
        xc = acc - mu
        var = jnp.mean(xc * xc, axis=-1, keepdims=True)
        y = xc * lax.rsqrt(var + EPS) * lg + lb
        ycv_ref[slot] = (y * _sigmoid(y)).astype(BF16)

    def project(blk, slot):
        rows = pl.ds(pl.multiple_of(blk * MIX_ROW_BLOCK, MIX_ROW_BLOCK), MIX_ROW_BLOCK)
        proj = (jnp.dot(yna_ref[0, rows, :], w_ref[0:d_attn, :], preferred_element_type=F32)
                + jnp.dot(ycv_ref[slot], w_ref[d_attn:, :], preferred_element_type=F32))
        o_ref[0, rows, :] = x_ref[0, rows, :] + gate * proj

    n_blocks = tm // MIX_ROW_BLOCK
    assert n_blocks >= 3
    partials(0, 0)
    partials(1, 1)
    conv_norm(0)
    conv_norm(1)
    partials(2, 0)
    for b in range(n_blocks):
        project(b, b % 2)
        if b + 2 < n_blocks:
            conv_norm(b % 2)
        if b + 3 < n_blocks:
            partials(b + 3, (b + 1) % 2)


def _mix_out(y_na, u, x, mod3, conv_w, conv_b, ln_g, ln_b, w_out, w_up, w_down, tm):
    B, S, D = x.shape
    d_conv = u.shape[-1]
    hb = tm // CONV_HALO
    n_hb = S // CONV_HALO
    cw_rep = jnp.broadcast_to(conv_w.astype(BF16)[:, None, :], (CONV_WIDTH, CONV_SPLIT, d_conv))
    shift = _shift_sum_matrix()
    n_steps = S // tm
    d_ff = w_down.shape[0]
    n_slabs = d_ff // FF_CHUNK
    assert B * n_steps >= n_slabs and w_up.shape[1] == 2 * d_ff
    slab = lambda b, i: jnp.minimum(b * n_steps + i, n_slabs - 1)
    return pl.pallas_call(
        functools.partial(_mix_out_kernel, tm=tm),
        grid=(B, S // tm),
        in_specs=[
            pl.BlockSpec((1, tm, D_ATTN), lambda b, i: (b, i, 0)),
            pl.BlockSpec((1, tm, d_conv), lambda b, i: (b, i, 0)),
            pl.BlockSpec((1, CONV_HALO, d_conv), lambda b, i: (b, jnp.maximum(i * hb - 1, 0), 0)),
            pl.BlockSpec((1, CONV_HALO, d_conv), lambda b, i: (b, jnp.minimum((i + 1) * hb, n_hb - 1), 0)),
            pl.BlockSpec((1, tm, D), lambda b, i: (b, i, 0)),
            pl.BlockSpec((1, 6, D), lambda b, i: (b, 0, 0)),
            _resident(cw_rep.shape),
            _resident((1, d_conv)),
            _resident((1, d_conv)),
            _resident((1, d_conv)),
            _resident(shift.shape),
            _resident(w_out.shape),
            pl.BlockSpec((D, 2 * FF_CHUNK), lambda b, i: (0, slab(b, i))),
            pl.BlockSpec((FF_CHUNK, D), lambda b, i: (slab(b, i), 0)),
        ],
        out_specs=[
            pl.BlockSpec((1, tm, D), lambda b, i: (b, i, 0)),
            pl.BlockSpec((D, 2 * FF_CHUNK), lambda b, i: (0, slab(b, i))),
            pl.BlockSpec((FF_CHUNK, D), lambda b, i: (slab(b, i), 0)),
        ],
        out_shape=[
            jax.ShapeDtypeStruct((B, S, D), F32),
            jax.ShapeDtypeStruct(w_up.shape, BF16),
            jax.ShapeDtypeStruct(w_down.shape, BF16),
        ],
        scratch_shapes=[
            pltpu.VMEM((tm + 2 * CONV_HALO, d_conv), BF16),
            pltpu.VMEM((MIX_SLOTS, shift.shape[1], d_conv), BF16),
            pltpu.VMEM((MIX_SLOTS, MIX_ROW_BLOCK, d_conv), BF16),
        ],
        compiler_params=pltpu.CompilerParams(
            dimension_semantics=("arbitrary", "arbitrary"), vmem_limit_bytes=VMEM_LIMIT_BYTES),
        name="conv_mixer_out_projection",
    )(y_na, u, u, u, x, mod3, cw_rep, conv_b, ln_g, ln_b, shift, w_out, w_up, w_down)


def _ffn_kernel(x_ref, xp_ref, xn_ref, mod_ref, g2_ref, wu_ref, fw_ref, fb_ref, wd_ref, gf_ref,
                o_ref, h_ref, act_ref, *, tm, d_ff):
    i = pl.program_id(1)
    last = pl.num_programs(1) - 1
    gain = g2_ref[...]
    shift = mod_ref[0, 3:4, :]
    scale = mod_ref[0, 4:5, :]
    hp = _norm_modulate(xp_ref[0], gain, shift, scale)
    hn = _norm_modulate(xn_ref[0], gain, shift, scale)
    h_ref[0:FFN_HALO] = jnp.where(i > 0, hp, 0.0).astype(BF16)
    h_ref[FFN_HALO:FFN_HALO + tm] = _norm_modulate(x_ref[0], gain, shift, scale).astype(BF16)
    h_ref[FFN_HALO + tm:] = jnp.where(i < last, hn, 0.0).astype(BF16)

    n_ext = tm + 2 * FFN_HALO

    def conv(lo):
        up = jnp.dot(h_ref[...], wu_ref[:, lo:lo + FF_CHUNK], preferred_element_type=F32)
        before = pltpu.roll(up, 1, axis=0)
        after = pltpu.roll(up, n_ext - 1, axis=0)
        mid = slice(FFN_HALO, FFN_HALO + tm)
        w = fw_ref[:, lo:lo + FF_CHUNK]
        return (w[0:1] * before[mid] + w[1:2] * up[mid] + w[2:3] * after[mid]
                + fb_ref[:, lo:lo + FF_CHUNK])

    for f in range(d_ff // FF_CHUNK):
        gate = conv(f * FF_CHUNK)
        val = conv(d_ff + f * FF_CHUNK)
        act_ref[:, f * FF_CHUNK:(f + 1) * FF_CHUNK] = (gate * _sigmoid(gate) * val).astype(BF16)

    for r in range(tm // FFN_DOWN_ROWS):
        rows = slice(r * FFN_DOWN_ROWS, (r + 1) * FFN_DOWN_ROWS)
        y = jnp.dot(act_ref[rows, :], wd_ref[...], preferred_element_type=F32)
        x2 = x_ref[0, rows, :] + mod_ref[0, 5:6, :] * y
        ms = jnp.mean(x2 * x2, axis=-1, keepdims=True)
        o_ref[0, rows, :] = x2 * lax.rsqrt(ms + EPS) * gf_ref[...]


def _conv_ffn(x1, mod3, g2, w_up, ffn_w, ffn_b, w_down, g_final, tm):
    B, S, D = x1.shape
    d_ff = w_down.shape[0]
    hb = tm // FFN_HALO
    n_hb = S // FFN_HALO
    return pl.pallas_call(
        functools.partial(_ffn_kernel, tm=tm, d_ff=d_ff),
        grid=(B, S // tm),
        in_specs=[
            pl.BlockSpec((1, tm, D), lambda b, i: (b, i, 0)),
            pl.BlockSpec((1, FFN_HALO, D), lambda b, i: (b, jnp.maximum(i * hb - 1, 0), 0)),
            pl.BlockSpec((1, FFN_HALO, D), lambda b, i: (b, jnp.minimum((i + 1) * hb, n_hb - 1), 0)),
            pl.BlockSpec((1, 6, D), lambda b, i: (b, 0, 0)),
            _resident((1, D)),
            _resident(w_up.shape),
            _resident(ffn_w.shape),
            _resident((1, 2 * d_ff)),
            _resident(w_down.shape),
            _resident((1, D)),
        ],
        out_specs=pl.BlockSpec((1, tm, D), lambda b, i: (b, i, 0)),
        out_shape=jax.ShapeDtypeStruct((B, S, D), F32),
        scratch_shapes=[
            pltpu.VMEM((tm + 2 * FFN_HALO, D), BF16),
            pltpu.VMEM((tm, d_ff), BF16),
        ],
        compiler_params=pltpu.CompilerParams(
            dimension_semantics=("arbitrary", "arbitrary"), vmem_limit_bytes=VMEM_LIMIT_BYTES),
        name="conv_ffn_final_norm",
    )(x1, x1, x1, mod3, g2, w_up, ffn_w, ffn_b, w_down, g_final)


def kernel(x, c, ctx, c_ctx, w_mod, b_mod, g_norm1, w_in, rpb, conv_w, conv_b, ln_g, ln_b, w_out,
           g_norm2, w_up, ffn_conv_w, ffn_conv_b, w_down, g_final):
    B, S, D = x.shape
    depth = w_mod.shape[0]
    assert depth == 1, "the context-stream update of deeper stacks is not implemented"
    assert S % GRID_W == 0 and D_ATTN + conv_w.shape[-1] == D
    l = 0
    row = lambda a: a.reshape(1, -1)

    mod_rows = 8
    cc = jnp.concatenate([c, c_ctx[None], jnp.zeros((mod_rows - B - 1, D), c.dtype)], axis=0)
    g1 = row(g_norm1[l])
    mod_first, k_c, v_c = _conditioning(cc, w_mod[l], row(b_mod[l]), ctx, g1, w_in[l], ctx_row=B)

    q, k, v, u, mod_rest = _input_projection(x, mod_first, g1, w_in[l], cc, w_mod[l], row(b_mod[l]),
                                             min(PROJ_TILE, S))
    mod3 = jnp.concatenate([mod_first, mod_rest.reshape(mod_rows, 4, D)], axis=1)

    y_na = _attention(q, k, v, k_c, v_c, rpb[l], rows_per_tile=ATTN_ROWS_PER_TILE)

    x1, w_up_b, w_down_b = _mix_out(
        y_na, u, x, mod3, conv_w[l], row(conv_b[l]), row(ln_g[l]), row(ln_b[l]),
        w_out[l].astype(BF16), w_up[l], w_down[l], min(MIX_TILE, S))

    return _conv_ffn(x1, mod3, row(g_norm2[l]), w_up_b, ffn_conv_w[l],
                     row(ffn_conv_b[l]), w_down_b, row(g_final), min(PROJ_TILE, S))
```
